```python
import functools
import jax, jax.numpy as jnp
from jax import lax
import numpy as np

D_MODEL = 1024
BATCH = 8
SEQ = 2048
DEPTH = 1

DN_HEADS = 8
DN_HEAD_DIM = 128
DN_DIM = DN_HEADS * DN_HEAD_DIM
DN_CHUNK = 64
CONV_WIDTH = 4
SGU_GROUPS = 8
SGU_GROUP_DIM = 128
SGU_DIM = SGU_GROUPS * SGU_GROUP_DIM
SGU_CHUNK = 128
N_GROUPS = 4
EXPERTS_PER_GROUP = 4
N_EXPERTS = N_GROUPS * EXPERTS_PER_GROUP
TOP_K_IN_GROUP = 2
EXPERT_FF = 256
DEEPNORM_ALPHA = (2.0 * DEPTH) ** 0.25
DEEPNORM_BETA = (8.0 * DEPTH) ** -0.25
LN_EPS = 1e-5
NORM_EPS = 1e-6
PROJ_SIZES = (3 * DN_DIM, DN_DIM, DN_HEADS, DN_HEADS, 2 * SGU_DIM, D_MODEL, D_MODEL)
PROJ_DIM = sum(PROJ_SIZES)

kernel_name = "hybrid_deltanet_sgu_hmoe_deepnorm"


def layer_norm(x, g, b):
    xf = x.astype(jnp.float32)
    xc = xf - jnp.mean(xf, axis=-1, keepdims=True)
    var = jnp.mean(xc * xc, axis=-1, keepdims=True)
    return (xc * lax.rsqrt(var + LN_EPS) * g + b).astype(x.dtype)


def l2_normalize(x):
    return x * lax.rsqrt(jnp.sum(x * x, axis=-1, keepdims=True) + NORM_EPS)


def causal_depthwise_conv(x, w):
    k = w.shape[0]
    return lax.conv_general_dilated(
        x, w[:, None, :], window_strides=(1,), padding=[(k - 1, 0)],
        dimension_numbers=("NWC", "WIO", "NWC"), feature_group_count=x.shape[-1])


def gated_delta_rule(q, k, v, g, beta):
    bsz, seq, heads, dk = q.shape
    dv = v.shape[-1]
    n = seq // DN_CHUNK

    def to_chunks(t):
        return jnp.swapaxes(t.reshape(bsz, n, DN_CHUNK, heads, *t.shape[3:]), 2, 3)

    q, k, v, g, beta = (to_chunks(t) for t in (q, k, v, g, beta))
    g_cum = jnp.cumsum(g, axis=-1)
    idx = jnp.arange(DN_CHUNK)
    causal = idx[:, None] >= idx[None, :]
    strict = idx[:, None] > idx[None, :]
    diff = g_cum[..., :, None] - g_cum[..., None, :]
    decay = jnp.where(causal, jnp.exp(jnp.where(causal, diff, 0.0)), 0.0)
    k_beta = k * beta[..., None]
    v_beta = v * beta[..., None]
    a_mat = jnp.where(strict, jnp.einsum("bnhtk,bnhsk->bnhts", k_beta, k) * decay, 0.0) \
        + jnp.eye(DN_CHUNK, dtype=q.dtype)
    solve = functools.partial(lax.linalg.triangular_solve, left_side=True, lower=True,
                              unit_diagonal=True)
    u = solve(a_mat, v_beta)
    w = solve(a_mat, k_beta * jnp.exp(g_cum)[..., None])
    qk = jnp.einsum("bnhtk,bnhsk->bnhts", q, k) * decay
    q_dec = q * jnp.exp(g_cum)[..., None]
    k_dec = k * jnp.exp(g_cum[..., -1:] - g_cum)[..., None]
    g_last = jnp.exp(g_cum[..., -1])

    def step(state, inp):
        qk_c, u_c, w_c, q_c, k_c, gl_c = inp
        v_new = u_c - jnp.einsum("bhtk,bhkv->bhtv", w_c, state)
        out = jnp.einsum("bhtk,bhkv->bhtv", q_c, state) + jnp.einsum("bhts,bhsv->bhtv", qk_c, v_new)
        state = state * gl_c[..., None, None] + jnp.einsum("bhsk,bhsv->bhkv", k_c, v_new)
        return state, out

    xs = tuple(jnp.moveaxis(t, 1, 0) for t in (qk, u, w, q_dec, k_dec, g_last))
    state0 = jnp.zeros((bsz, heads, dk, dv), jnp.float32)
    _, out = lax.scan(step, state0, xs)
    return jnp.transpose(out, (1, 0, 3, 2, 4)).reshape(bsz, seq, heads, dv)


def spatial_gating(uv, ln_g, ln_b, spatial_w, spatial_b, w_sgu_out):
    bsz, seq, _ = uv.shape
    n = seq // SGU_CHUNK
    u, v = jnp.split(jax.nn.gelu(uv, approximate=False), 2, axis=-1)
    v = layer_norm(v, ln_g, ln_b)
    v = v.reshape(bsz, n, SGU_CHUNK, SGU_GROUPS, SGU_GROUP_DIM)
    idx = jnp.arange(SGU_CHUNK)
    causal = idx[:, None] >= idx[None, :]
    w_causal = jnp.where(causal, spatial_w, 0)
    v = jnp.einsum("gts,bnsgc->bntgc", w_causal, v) + spatial_b.T[:, :, None]
    gated = u * v.reshape(bsz, seq, SGU_DIM)
    return jnp.einsum("blc,cd->bld", gated, w_sgu_out)


def hybrid_mixer(x, w_in, conv_w, a_log, dt_bias, dn_norm_g, w_dn_out,
                 sgu_ln_g, sgu_ln_b, spatial_w, spatial_b, w_sgu_out, w_out):
    bsz, seq, _ = x.shape
    f32 = jnp.float32
    proj = jnp.einsum("bld,dp->blp", x, w_in)
    offsets = np.cumsum(PROJ_SIZES)[:-1].tolist()
    qkv, z, a, b, uv, gate_dn, gate_sgu = jnp.split(proj, offsets, axis=-1)

    qkv = jax.nn.silu(causal_depthwise_conv(qkv, conv_w)).astype(f32)
    q, k, v = jnp.split(qkv, 3, axis=-1)
    to_heads = lambda t: t.reshape(bsz, seq, DN_HEADS, DN_HEAD_DIM)
    q = l2_normalize(to_heads(q)) * (DN_HEAD_DIM ** -0.5)
    k = l2_normalize(to_heads(k))
    v = to_heads(v)
    beta = jax.nn.sigmoid(b.astype(f32))
    g = -jnp.exp(a_log.astype(f32)) * jax.nn.softplus(a.astype(f32) + dt_bias.astype(f32))
    o = gated_delta_rule(q, k, v, g, beta)
    o = o * lax.rsqrt(jnp.mean(o * o, axis=-1, keepdims=True) + NORM_EPS) * dn_norm_g.astype(f32)
    o = o * jax.nn.silu(to_heads(z).astype(f32))
    y_dn = jnp.einsum("blc,cd->bld", o.reshape(bsz, seq, DN_DIM).astype(x.dtype), w_dn_out)

    y_sgu = spatial_gating(uv, sgu_ln_g, sgu_ln_b, spatial_w, spatial_b, w_sgu_out)

    y = jax.nn.sigmoid(gate_dn) * y_dn + jax.nn.sigmoid(gate_sgu) * y_sgu
    return jnp.einsum("bld,de->ble", y, w_out)


def hierarchical_moe(x, router_group_w, router_group_b, router_expert_w, router_expert_b,
                     expert_w_gate, expert_w_up, expert_w_down):
    bsz, seq, d = x.shape
    f32 = jnp.float32
    t = x.reshape(bsz * seq, d)
    n_tok = t.shape[0]
    group_logits = (t @ router_group_w + router_group_b).astype(f32)
    group_p, group_idx = lax.top_k(jax.nn.softmax(group_logits, axis=-1), 1)
    group_p, group_idx = group_p[:, 0], group_idx[:, 0]
    expert_logits = (t @ router_expert_w + router_expert_b).astype(f32)
    expert_logits = expert_logits.reshape(n_tok, N_GROUPS, EXPERTS_PER_GROUP)
    in_group = expert_logits[jnp.arange(n_tok), group_idx]
    exp_p, exp_idx = lax.top_k(jax.nn.softmax(in_group, axis=-1), TOP_K_IN_GROUP)
    exp_p = exp_p / jnp.sum(exp_p, axis=-1, keepdims=True)
    gate = group_p[:, None] * exp_p
    expert_id = group_idx[:, None] * EXPERTS_PER_GROUP + exp_idx
    combine = jnp.sum(jax.nn.one_hot(expert_id, N_EXPERTS, dtype=f32) * gate[..., None], axis=1)
    h = jax.nn.silu(jnp.einsum("td,edf->tef", t, expert_w_gate)) \
        * jnp.einsum("td,edf->tef", t, expert_w_up)
    h = h * combine[:, :, None].astype(h.dtype)
    y = jnp.einsum("tef,efd->td", h, expert_w_down)
    return y.reshape(bsz, seq, d)


def setup_inputs(seed: int = 0) -> dict:
    key = jax.random.key(seed)
    ks = jax.random.split(key, 24)
    f32 = jnp.float32
    nrm = lambda k, shape, scale: jax.random.normal(k, shape, f32) * scale
    dt = jnp.exp(jax.random.uniform(ks[4], (DEPTH, DN_HEADS), f32,
                                    minval=float(np.log(1e-3)), maxval=float(np.log(1e-1))))
    return {
        "x": nrm(ks[0], (BATCH, SEQ, D_MODEL), 1.0),
        "w_in": nrm(ks[1], (DEPTH, D_MODEL, PROJ_DIM), D_MODEL ** -0.5),
        "conv_w": nrm(ks[2], (DEPTH, CONV_WIDTH, 3 * DN_DIM), CONV_WIDTH ** -0.5),
        "a_log": jnp.log(jax.random.uniform(ks[3], (DEPTH, DN_HEADS), f32, minval=1.0, maxval=16.0)),
        "dt_bias": dt + jnp.log(-jnp.expm1(-dt)),
        "dn_norm_g": 1.0 + nrm(ks[5], (DEPTH, DN_HEAD_DIM), 0.1),
        "w_dn_out": nrm(ks[6], (DEPTH, DN_DIM, D_MODEL), DN_DIM ** -0.5),
        "sgu_ln_g": 1.0 + nrm(ks[7], (DEPTH, SGU_DIM), 0.1),
        "sgu_ln_b": nrm(ks[8], (DEPTH, SGU_DIM), 0.02),
        "spatial_w": nrm(ks[9], (DEPTH, SGU_GROUPS, SGU_CHUNK, SGU_CHUNK), 0.05),
        "spatial_b": 1.0 + nrm(ks[10], (DEPTH, SGU_GROUPS, SGU_CHUNK), 0.1),
        "w_sgu_out": nrm(ks[11], (DEPTH, SGU_DIM, D_MODEL), SGU_DIM ** -0.5),
        "w_out": nrm(ks[12], (DEPTH, D_MODEL, D_MODEL), D_MODEL ** -0.5 * DEEPNORM_BETA),
        "ln1_g": 1.0 + nrm(ks[13], (DEPTH, D_MODEL), 0.1),
        "ln1_b": nrm(ks[14], (DEPTH, D_MODEL), 0.02),
        "router_group_w": nrm(ks[15], (DEPTH, D_MODEL, N_GROUPS), D_MODEL ** -0.5),
        "router_group_b": nrm(ks[16], (DEPTH, N_GROUPS), 0.01),
        "router_expert_w": nrm(ks[17], (DEPTH, D_MODEL, N_EXPERTS), D_MODEL ** -0.5),
        "router_expert_b": nrm(ks[18], (DEPTH, N_EXPERTS), 0.01),
        "expert_w_gate": nrm(ks[19], (DEPTH, N_EXPERTS, D_MODEL, EXPERT_FF), D_MODEL ** -0.5),
        "expert_w_up": nrm(ks[20], (DEPTH, N_EXPERTS, D_MODEL, EXPERT_FF), D_MODEL ** -0.5),
        "expert_w_down": nrm(ks[21], (DEPTH, N_EXPERTS, EXPERT_FF, D_MODEL), EXPERT_FF ** -0.5 * DEEPNORM_BETA),
        "ln2_g": 1.0 + nrm(ks[22], (DEPTH, D_MODEL), 0.1),
        "ln2_b": nrm(ks[23], (DEPTH, D_MODEL), 0.02),
    }


def reference(x, w_in, conv_w, a_log, dt_bias, dn_norm_g, w_dn_out, sgu_ln_g, sgu_ln_b,
              spatial_w, spatial_b, w_sgu_out, w_out, ln1_g, ln1_b,
              router_group_w, router_group_b, router_expert_w, router_expert_b,
              expert_w_gate, expert_w_up, expert_w_down, ln2_g, ln2_b):
    h = x
    for l in range(DEPTH):
        mix = hybrid_mixer(h, w_in[l], conv_w[l], a_log[l], dt_bias[l], dn_norm_g[l], w_dn_out[l],
                           sgu_ln_g[l], sgu_ln_b[l], spatial_w[l], spatial_b[l], w_sgu_out[l], w_out[l])
        h = layer_norm(DEEPNORM_ALPHA * h + mix, ln1_g[l], ln1_b[l])
        ffn = hierarchical_moe(h, router_group_w[l], router_group_b[l], router_expert_w[l],
                               router_expert_b[l], expert_w_gate[l], expert_w_up[l], expert_w_down[l])
        h = layer_norm(DEEPNORM_ALPHA * h + ffn, ln2_g[l], ln2_b[l])
    return h
```

```python
import functools

import jax
import jax.numpy as jnp
from jax import lax
from jax.experimental import pallas as pl
from jax.experimental.pallas import tpu as pltpu

F32 = jnp.float32
BF16 = jnp.bfloat16

LANES = 128
DN_HEADS = 8
DN_HEAD_DIM = 128
CONV_WIDTH = 4
SGU_GROUPS = 8
SGU_CHUNK = 128
N_GROUPS = 4
EXPERTS_PER_GROUP = 4
N_EXPERTS = N_GROUPS * EXPERTS_PER_GROUP
LN_EPS = 1e-5
NORM_EPS = 1e-6
DELTA_CHUNK = 128
CONV_PAD = 8
VMEM_LIMIT = 56 * 1024 * 1024

NT_DIMS = (((1,), (1,)), ((), ()))
TN_DIMS = (((0,), (0,)), ((), ()))


def _dot(a, b):
    return jnp.dot(a.astype(BF16), b.astype(BF16), preferred_element_type=F32)


def _dot_dims(a, b, dims):
    return lax.dot_general(a.astype(BF16), b.astype(BF16), dims, preferred_element_type=F32)


def _dot_f32(a, b):
    return jnp.dot(a, b, preferred_element_type=F32, precision=lax.Precision.HIGHEST)


def _sigmoid(x):
    return 1.0 / (1.0 + jnp.exp(-x))


def _silu(x):
    return x * _sigmoid(x)


def _const_spec(shape):
    nd = len(shape)
    return pl.BlockSpec(shape, lambda *_: (0,) * nd, pipeline_mode=pl.Buffered(1))


def _params(n_axes):
    return pltpu.CompilerParams(dimension_semantics=("arbitrary",) * n_axes,
                                vmem_limit_bytes=VMEM_LIMIT)


def _proj_dn_kernel(x_ref, wqkv_ref, wz_ref, wab_ref, conv_ref, gpar_ref,
                    qkv_ref, sz_ref, gb_ref, ext_ref, *, tm, dn_dim):
    i = pl.program_id(1)
    xb = x_ref[...].astype(BF16)
    n_qkv = 3 * dn_dim

    @pl.when(i == 0)
    def _():
        ext_ref[0:CONV_PAD, :] = jnp.zeros((CONV_PAD, n_qkv), F32)

    for n in range(0, n_qkv, 256):
        ext_ref[CONV_PAD:CONV_PAD + tm, n:n + 256] = jnp.dot(
            xb, wqkv_ref[:, n:n + 256], preferred_element_type=F32)

    for n in range(0, n_qkv, LANES):
        w = conv_ref[:, n:n + LANES]
        y = w[3:4, :] * ext_ref[CONV_PAD:CONV_PAD + tm, n:n + LANES]
        for j in range(1, CONV_WIDTH):
            y = y + w[3 - j:4 - j, :] * ext_ref[CONV_PAD - j:CONV_PAD - j + tm, n:n + LANES]
        y = _silu(y)
        if n < 2 * dn_dim:
            y = y * lax.rsqrt(jnp.sum(y * y, axis=-1, keepdims=True) + NORM_EPS)
            if n < dn_dim:
                y = y * (DN_HEAD_DIM ** -0.5)
        qkv_ref[:, n:n + LANES] = y.astype(BF16)

    ext_ref[0:CONV_PAD, :] = ext_ref[tm:tm + CONV_PAD, :]

    for n in range(0, dn_dim, 256):
        z = jnp.dot(xb, wz_ref[:, n:n + 256], preferred_element_type=F32)
        sz_ref[:, n:n + 256] = _silu(z).astype(BF16)

    ab = jnp.dot(xb, wab_ref[...], preferred_element_type=F32)
    neg_exp_alog = gpar_ref[0:1, :]
    dt_bias = gpar_ref[1:2, :]
    sp_in = ab + dt_bias
    softplus = jnp.maximum(sp_in, 0.0) + jnp.log1p(jnp.exp(-jnp.abs(sp_in)))
    lane = lax.broadcasted_iota(jnp.int32, ab.shape, 1)
    gb_ref[...] = jnp.where(lane < DN_HEADS, neg_exp_alog * softplus, _sigmoid(ab))


def _proj_dn(x2, wqkv, wz, wab, conv_w, gpar, *, bsz, seq, tm):
    d = x2.shape[1]
    dn_dim = wz.shape[1]
    nl = seq // tm
    tok = lambda b, i: (b * nl + i, 0)
    return pl.pallas_call(
        functools.partial(_proj_dn_kernel, tm=tm, dn_dim=dn_dim),
        grid=(bsz, nl),
        in_specs=[pl.BlockSpec((tm, d), tok),
                  _const_spec(wqkv.shape), _const_spec(wz.shape), _const_spec(wab.shape),
                  _const_spec(conv_w.shape), _const_spec(gpar.shape)],
        out_specs=[pl.BlockSpec((tm, 3 * dn_dim), tok),
                   pl.BlockSpec((tm, dn_dim), tok),
                   pl.BlockSpec((tm, LANES), tok)],
        out_shape=[jax.ShapeDtypeStruct((bsz * seq, 3 * dn_dim), BF16),
                   jax.ShapeDtypeStruct((bsz * seq, dn_dim), BF16),
                   jax.ShapeDtypeStruct((bsz * seq, LANES), F32)],
        scratch_shapes=[pltpu.VMEM((tm + CONV_PAD, 3 * dn_dim), F32)],
        compiler_params=_params(2),
        name="proj_dn",
    )(x2, wqkv, wz, wab, conv_w, gpar)


def _delta_kernel(q_ref, k_ref, v_ref, gb_ref, sz_ref, ng_ref, o_ref, state_ref, *, tc):
    c = DELTA_CHUNK

    @pl.when(pl.program_id(1) == 0)
    def _():
        state_ref[...] = jnp.zeros(state_ref.shape, F32)

    row = lax.broadcasted_iota(jnp.int32, (c, c), 0)
    col = lax.broadcasted_iota(jnp.int32, (c, c), 1)
    causal = row >= col
    strict = row > col
    tri = causal.astype(F32)
    eye = (row == col).astype(F32)
    norm_g = ng_ref[...]

    for ci in range(tc // c):
        rows = slice(ci * c, (ci + 1) * c)
        gb = gb_ref[rows, :]
        gcum = _dot_f32(tri, gb)
        gcum_t = gcum.T
        for h in range(DN_HEADS):
            cols = slice(h * DN_HEAD_DIM, (h + 1) * DN_HEAD_DIM)
            q = q_ref[rows, cols].astype(F32)
            k = k_ref[rows, cols].astype(F32)
            v = v_ref[rows, cols].astype(F32)
            g_col = gcum[:, h:h + 1]
            g_row = gcum_t[h:h + 1, :]
            g_last = gcum[c - 1:c, h:h + 1]
            beta = gb[:, DN_HEADS + h:DN_HEADS + h + 1]
            decay = jnp.where(causal, jnp.exp(jnp.where(causal, g_col - g_row, 0.0)), 0.0)
            exp_g = jnp.exp(g_col)
            k_beta = k * beta
            a_mat = jnp.where(strict, _dot_dims(k_beta, k, NT_DIMS) * decay, 0.0)
            p = -a_mat
            t_inv = eye + p
            for _ in range(c.bit_length() - 2):
                p = _dot_f32(p, p)
                t_inv = t_inv + _dot_f32(t_inv, p)
            u = _dot(t_inv, v * beta)
            w = _dot(t_inv, k_beta * exp_g)
            qk = _dot_dims(q, k, NT_DIMS) * decay
            state = state_ref[h]
            v_new = u - _dot(w, state)
            out = _dot(q * exp_g, state) + _dot(qk, v_new)
            k_dec = k * jnp.exp(g_last - g_col)
            state_ref[h] = state * jnp.exp(g_last) + _dot_dims(k_dec, v_new, TN_DIMS)
            out = out * lax.rsqrt(jnp.mean(out * out, axis=-1, keepdims=True) + NORM_EPS) * norm_g
            out = out * sz_ref[rows, cols].astype(F32)
            o_ref[rows, cols] = out.astype(BF16)


def _delta(qkv, sz, gb, norm_g, *, bsz, seq, tc):
    dn_dim = sz.shape[1]
    nl = seq // tc
    tok = lambda b, i: (b * nl + i, 0)
    n_col_blocks = 1
    return pl.pallas_call(
        functools.partial(_delta_kernel, tc=tc),
        grid=(bsz, nl),
        in_specs=[pl.BlockSpec((tc, dn_dim), lambda b, i: (b * nl + i, 0)),
                  pl.BlockSpec((tc, dn_dim), lambda b, i: (b * nl + i, n_col_blocks)),
                  pl.BlockSpec((tc, dn_dim), lambda b, i: (b * nl + i, 2 * n_col_blocks)),
                  pl.BlockSpec((tc, LANES), tok),
                  pl.BlockSpec((tc, dn_dim), tok),
                  _const_spec(norm_g.shape)],
        out_specs=pl.BlockSpec((tc, dn_dim), tok),
        out_shape=jax.ShapeDtypeStruct((bsz * seq, dn_dim), BF16),
        scratch_shapes=[pltpu.VMEM((DN_HEADS, DN_HEAD_DIM, DN_HEAD_DIM), F32)],
        compiler_params=_params(2),
        name="delta",
    )(qkv, qkv, qkv, gb, sz, norm_g)


def _sgu_kernel(x_ref, wuv_ref, wgate_ref, lng_ref, lnb_ref, sw_ref, sb_ref, wso_ref,
                ys_ref, sgd_ref, u_ref, v_ref, gated_ref, *, tm, sgu_dim, d_model):
    xb = x_ref[...].astype(BF16)
    inv_sqrt2 = 2.0 ** -0.5
    for n in range(0, 2 * sgu_dim, 256):
        uv = jnp.dot(xb, wuv_ref[:, n:n + 256], preferred_element_type=F32)
        act = 0.5 * uv * (1.0 + lax.erf(uv * inv_sqrt2))
        if n < sgu_dim:
            u_ref[:, n:n + 256] = act
        else:
            v_ref[:, n - sgu_dim:n - sgu_dim + 256] = act

    v = v_ref[...]
    vc = v - jnp.mean(v, axis=-1, keepdims=True)
    var = jnp.mean(vc * vc, axis=-1, keepdims=True)
    v_ref[...] = vc * lax.rsqrt(var + LN_EPS) * lng_ref[...] + lnb_ref[...]

    row = lax.broadcasted_iota(jnp.int32, (SGU_CHUNK, SGU_CHUNK), 0)
    col = lax.broadcasted_iota(jnp.int32, (SGU_CHUNK, SGU_CHUNK), 1)
    causal = row >= col
    for g in range(SGU_GROUPS):
        cols = slice(g * LANES, (g + 1) * LANES)
        w_causal = jnp.where(causal, sw_ref[g], 0.0).astype(BF16)
        bias = sb_ref[:, g:g + 1]
        for ci in range(tm // SGU_CHUNK):
            rows = slice(ci * SGU_CHUNK, (ci + 1) * SGU_CHUNK)
            mixed = jnp.dot(w_causal, v_ref[rows, cols].astype(BF16), preferred_element_type=F32) + bias
            gated_ref[rows, cols] = (u_ref[rows, cols] * mixed).astype(BF16)

    gated = gated_ref[...]
    for n in range(0, d_model, 256):
        y_sgu = jnp.dot(gated, wso_ref[:, n:n + 256], preferred_element_type=F32)
        gate_sgu = jnp.dot(xb, wgate_ref[:, d_model + n:d_model + n + 256], preferred_element_type=F32)
        ys_ref[:, n:n + 256] = (_sigmoid(gate_sgu) * y_sgu).astype(BF16)
        gate_dn = jnp.dot(xb, wgate_ref[:, n:n + 256], preferred_element_type=F32)
        sgd_ref[:, n:n + 256] = _sigmoid(gate_dn).astype(BF16)


def _sgu(x2, wuv, wgate, ln_g, ln_b, spatial_w, spatial_b_t, wso, *, tm):
    t, d = x2.shape
    sgu_dim = wso.shape[0]
    tok = lambda i: (i, 0)
    return pl.pallas_call(
        functools.partial(_sgu_kernel, tm=tm, sgu_dim=sgu_dim, d_model=d),
        grid=(t // tm,),
        in_specs=[pl.BlockSpec((tm, d), tok),
                  _const_spec(wuv.shape), _const_spec(wgate.shape),
                  _const_spec(ln_g.shape), _const_spec(ln_b.shape),
                  _const_spec(spatial_w.shape), _const_spec(spatial_b_t.shape),
                  _const_spec(wso.shape)],
        out_specs=[pl.BlockSpec((tm, d), tok), pl.BlockSpec((tm, d), tok)],
        out_shape=[jax.ShapeDtypeStruct((t, d), BF16), jax.ShapeDtypeStruct((t, d), BF16)],
        scratch_shapes=[pltpu.VMEM((tm, sgu_dim), F32), pltpu.VMEM((tm, sgu_dim), F32),
                        pltpu.VMEM((tm, sgu_dim), BF16)],
        compiler_params=_params(1),
        name="sgu",
    )(x2, wuv, wgate, ln_g, ln_b, spatial_w, spatial_b_t, wso)


def _layer_norm(r, g, b):
    rc = r - jnp.mean(r, axis=-1, keepdims=True)
    var = jnp.mean(rc * rc, axis=-1, keepdims=True)
    return rc * lax.rsqrt(var + LN_EPS) * g + b


def _merge_kernel(x_ref, o_ref, sgd_ref, ys_ref, wdn_ref, wout_ref, g_ref, b_ref, h_ref, *, alpha):
    y_dn = jnp.dot(o_ref[...], wdn_ref[...], preferred_element_type=F32)
    y = sgd_ref[...].astype(F32) * y_dn + ys_ref[...].astype(F32)
    mix = jnp.dot(y.astype(BF16), wout_ref[...], preferred_element_type=F32)
    h_ref[...] = _layer_norm(alpha * x_ref[...] + mix, g_ref[...], b_ref[...])


def _merge(x2, o, sgd, ys, wdn, wout, ln_g, ln_b, *, tm, alpha):
    t, d = x2.shape
    tok = lambda i: (i, 0)
    return pl.pallas_call(
        functools.partial(_merge_kernel, alpha=alpha),
        grid=(t // tm,),
        in_specs=[pl.BlockSpec((tm, d), tok), pl.BlockSpec((tm, o.shape[1]), tok),
                  pl.BlockSpec((tm, d), tok), pl.BlockSpec((tm, d), tok),
                  _const_spec(wdn.shape), _const_spec(wout.shape),
                  _const_spec(ln_g.shape), _const_spec(ln_b.shape)],
        out_specs=pl.BlockSpec((tm, d), tok),
        out_shape=jax.ShapeDtypeStruct((t, d), F32),
        compiler_params=_params(1),
        name="merge",
    )(x2, o, sgd, ys, wdn, wout, ln_g, ln_b)


def _moe_kernel(h_ref, wr_ref, br_ref, wg_ref, wu_ref, wd_ref, g_ref, b_ref, out_ref, hm_ref,
                *, alpha, expert_ff):
    h = h_ref[...]
    hb = h.astype(BF16)
    neg_inf = -jnp.inf
    logits = _dot_f32(h, wr_ref[...]) + br_ref[...]
    lane = lax.broadcasted_iota(jnp.int32, logits.shape, 1)
    big = jnp.int32(LANES)

    g_logit = jnp.where(lane < N_GROUPS, logits, neg_inf)
    g_max = jnp.max(g_logit, axis=-1, keepdims=True)
    g_idx = jnp.min(jnp.where(g_logit == g_max, lane, big), axis=-1, keepdims=True)
    group_p = 1.0 / jnp.sum(jnp.exp(g_logit - g_max), axis=-1, keepdims=True)

    lo = N_GROUPS + EXPERTS_PER_GROUP * g_idx
    e_logit = jnp.where(lane >= lo, jnp.where(lane < lo + EXPERTS_PER_GROUP, logits, neg_inf), neg_inf)
    e1 = jnp.max(e_logit, axis=-1, keepdims=True)
    i1 = jnp.min(jnp.where(e_logit == e1, lane, big), axis=-1, keepdims=True)
    e_rest = jnp.where(lane == i1, neg_inf, e_logit)
    e2 = jnp.max(e_rest, axis=-1, keepdims=True)
    i2 = jnp.min(jnp.where(e_rest == e2, lane, big), axis=-1, keepdims=True)
    r = jnp.exp(e2 - e1)
    p1 = group_p / (1.0 + r)
    p2 = group_p * r / (1.0 + r)
    combine = jnp.where(lane == i1, p1, jnp.where(lane == i2, p2, 0.0))

    for e in range(N_EXPERTS):
        hg = jnp.dot(hb, wg_ref[e], preferred_element_type=F32)
        hu = jnp.dot(hb, wu_ref[e], preferred_element_type=F32)
        c_e = combine[:, N_GROUPS + e:N_GROUPS + e + 1]
        hm_ref[:, e * expert_ff:(e + 1) * expert_ff] = (_silu(hg) * hu * c_e).astype(BF16)

    ffn = jnp.dot(hm_ref[...], wd_ref[...], preferred_element_type=F32)
    out_ref[...] = _layer_norm(alpha * h + ffn, g_ref[...], b_ref[...])


def _moe(h1, wr, br, wg, wu, wd, ln_g, ln_b, *, tm, alpha):
    t, d = h1.shape
    expert_ff = wg.shape[2]
    tok = lambda i: (i, 0)
    return pl.pallas_call(
        functools.partial(_moe_kernel, alpha=alpha, expert_ff=expert_ff),
        grid=(t // tm,),
        in_specs=[pl.BlockSpec((tm, d), tok),
                  _const_spec(wr.shape), _const_spec(br.shape),
                  _const_spec(wg.shape), _const_spec(wu.shape), _const_spec(wd.shape),
                  _const_spec(ln_g.shape), _const_spec(ln_b.shape)],
        out_specs=pl.BlockSpec((tm, d), tok),
        out_shape=jax.ShapeDtypeStruct((t, d), F32),
        scratch_shapes=[pltpu.VMEM((tm, N_EXPERTS * expert_ff), BF16)],
        compiler_params=_params(1),
        name="moe",
    )(h1, wr, br, wg, wu, wd, ln_g, ln_b)


def _pad_lanes(a, width=LANES):
    return jnp.pad(a, [(0, 0)] * (a.ndim - 1) + [(0, width - a.shape[-1])])


def _layer(h, w_in, conv_w, a_log, dt_bias, dn_norm_g, w_dn_out, sgu_ln_g, sgu_ln_b,
           spatial_w, spatial_b, w_sgu_out, w_out, ln1_g, ln1_b,
           router_group_w, router_group_b, router_expert_w, router_expert_b,
           expert_w_gate, expert_w_up, expert_w_down, ln2_g, ln2_b, *, alpha):
    bsz, seq, d = h.shape
    dn_dim = w_dn_out.shape[0]
    sgu_dim = w_sgu_out.shape[0]
    x2 = h.reshape(bsz * seq, d)

    o_z = 3 * dn_dim
    o_a = o_z + dn_dim
    o_uv = o_a + 2 * DN_HEADS
    o_gate = o_uv + 2 * sgu_dim
    wqkv = w_in[:, :o_z].astype(BF16)
    wz = w_in[:, o_z:o_a].astype(BF16)
    wab = _pad_lanes(w_in[:, o_a:o_uv]).astype(BF16)
    wuv = w_in[:, o_uv:o_gate].astype(BF16)
    wgate = w_in[:, o_gate:].astype(BF16)
    gpar = jnp.stack([_pad_lanes(-jnp.exp(a_log.astype(F32))), _pad_lanes(dt_bias.astype(F32))])

    qkv, sz, gb = _proj_dn(x2, wqkv, wz, wab, conv_w, gpar, bsz=bsz, seq=seq, tm=256)
    o = _delta(qkv, sz, gb, dn_norm_g.reshape(1, -1), bsz=bsz, seq=seq, tc=256)
    ys, sgd = _sgu(x2, wuv, wgate, sgu_ln_g.reshape(1, -1), sgu_ln_b.reshape(1, -1),
                   spatial_w, spatial_b.T, w_sgu_out.astype(BF16), tm=256)
    h1 = _merge(x2, o, sgd, ys, w_dn_out.astype(BF16), w_out.astype(BF16),
                ln1_g.reshape(1, -1), ln1_b.reshape(1, -1), tm=512, alpha=alpha)

    wr = _pad_lanes(jnp.concatenate([router_group_w, router_expert_w], axis=1))
    br = _pad_lanes(jnp.concatenate([router_group_b, router_expert_b]).reshape(1, -1))
    wd = expert_w_down.astype(BF16).reshape(-1, d)
    h2 = _moe(h1, wr, br, expert_w_gate.astype(BF16), expert_w_up.astype(BF16), wd,
              ln2_g.reshape(1, -1), ln2_b.reshape(1, -1), tm=256, alpha=alpha)
    return h2.reshape(bsz, seq, d)


def kernel(x, w_in, conv_w, a_log, dt_bias, dn_norm_g, w_dn_out, sgu_ln_g, sgu_ln_b, spatial_w, spatial_b, w_sgu_out, w_out, ln1_g, ln1_b, router_group_w, router_group_b, router_expert_w, router_expert_b, expert_w_gate, expert_w_up, expert_w_down, ln2_g, ln2_b):
    depth = w_in.shape[0]
    alpha = (2.0 * depth) ** 0.25
    h = x
    for l in range(depth):
        h = _layer(h, w_in[l], conv_w[l], a_log[l], dt_bias[l], dn_norm_g[l], w_dn_out[l],
                   sgu_ln_g[l], sgu_ln_b[l], spatial_w[l], spatial_b[l], w_sgu_out[l], w_out[l],
                   ln1_g[l], ln1_b[l], router_group_w[l], router_group_b[l],
                   router_expert_w[l], router_expert_b[l],
                   expert_w_gate[l], expert_w_up[l], expert_w_down[l], ln2_g[l], ln2_b[l], alpha=alpha)
    return h
```

```python
import functools

import jax
import jax.numpy as jnp
from jax import lax
from jax.experimental import pallas as pl
from jax.experimental.pallas import tpu as pltpu

F32 = jnp.float32
BF16 = jnp.bfloat16

LANES = 128
DN_HEADS = 8
DN_HEAD_DIM = 128
CONV_WIDTH = 4
SGU_GROUPS = 8
SGU_CHUNK = 128
N_GROUPS = 4
EXPERTS_PER_GROUP = 4
N_EXPERTS = N_GROUPS * EXPERTS_PER_GROUP
LN_EPS = 1e-5
NORM_EPS = 1e-6
DELTA_CHUNK = 128
CONV_PAD = 8
VMEM_LIMIT = 56 * 1024 * 1024

NT_DIMS = (((1,), (1,)), ((), ()))
TN_DIMS = (((0,), (0,)), ((), ()))


def _dot_f32(a, b):
    return jnp.dot(a, b, preferred_element_type=F32, precision=lax.Precision.HIGHEST)


def _split(x):
    hi = x.astype(BF16)
    return hi, (x - hi.astype(F32)).astype(BF16)


def _dot3_rows(lhs, b_hi, b_lo):
    m = lhs[0].shape[0]
    n = len(lhs)
    parts = [_split(a) for a in lhs]
    his = [hi for hi, _ in parts]
    los = [lo for _, lo in parts]
    r_hi = jnp.dot(jnp.concatenate(his + los, axis=0), b_hi, preferred_element_type=F32)
    r_lo = jnp.dot(jnp.concatenate(his, axis=0) if n > 1 else his[0], b_lo, preferred_element_type=F32)
    return [r_hi[i * m:(i + 1) * m] + r_hi[(n + i) * m:(n + i + 1) * m] + r_lo[i * m:(i + 1) * m]
            for i in range(n)]


def _sigmoid(x):
    return 1.0 / (1.0 + jnp.exp(-x))


def _silu(x):
    return x * _sigmoid(x)


def _const_spec(shape):
    nd = len(shape)
    return pl.BlockSpec(shape, lambda *_: (0,) * nd, pipeline_mode=pl.Buffered(1))


def _params(n_axes):
    return pltpu.CompilerParams(dimension_semantics=("arbitrary",) * n_axes,
                                vmem_limit_bytes=VMEM_LIMIT)


def _proj_dn_kernel(x_ref, wqkv_ref, wz_ref, wab_ref, conv_ref, gpar_ref,
                    qkv_ref, sz_ref, gb_ref, ext_ref, *, tm, dn_dim):
    i = pl.program_id(1)
    xb = x_ref[...].astype(BF16)
    n_qkv = 3 * dn_dim

    @pl.when(i == 0)
    def _():
        ext_ref[0:CONV_PAD, :] = jnp.zeros((CONV_PAD, n_qkv), F32)

    for n in range(0, n_qkv, 256):
        ext_ref[CONV_PAD:CONV_PAD + tm, n:n + 256] = jnp.dot(
            xb, wqkv_ref[:, n:n + 256], preferred_element_type=F32)

    for n in range(0, n_qkv, LANES):
        w = conv_ref[:, n:n + LANES]
        y = w[3:4, :] * ext_ref[CONV_PAD:CONV_PAD + tm, n:n + LANES]
        for j in range(1, CONV_WIDTH):
            y = y + w[3 - j:4 - j, :] * ext_ref[CONV_PAD - j:CONV_PAD - j + tm, n:n + LANES]
        y = _silu(y)
        if n < 2 * dn_dim:
            y = y * lax.rsqrt(jnp.sum(y * y, axis=-1, keepdims=True) + NORM_EPS)
            if n < dn_dim:
                y = y * (DN_HEAD_DIM ** -0.5)
        qkv_ref[:, n:n + LANES] = y.astype(BF16)

    ext_ref[0:CONV_PAD, :] = ext_ref[tm:tm + CONV_PAD, :]

    for n in range(0, dn_dim, 256):
        z = jnp.dot(xb, wz_ref[:, n:n + 256], preferred_element_type=F32)
        sz_ref[:, n:n + 256] = _silu(z).astype(BF16)

    ab = jnp.dot(xb, wab_ref[...], preferred_element_type=F32)
    neg_exp_alog = gpar_ref[0:1, :]
    dt_bias = gpar_ref[1:2, :]
    sp_in = ab + dt_bias
    softplus = jnp.maximum(sp_in, 0.0) + jnp.log1p(jnp.exp(-jnp.abs(sp_in)))
    lane = lax.broadcasted_iota(jnp.int32, ab.shape, 1)
    gb_ref[...] = jnp.where(lane < DN_HEADS, neg_exp_alog * softplus, _sigmoid(ab))


def _proj_dn(x2, wqkv, wz, wab, conv_w, gpar, *, bsz, seq, tm):
    d = x2.shape[1]
    dn_dim = wz.shape[1]
    nl = seq // tm
    tok = lambda b, i: (b * nl + i, 0)
    return pl.pallas_call(
        functools.partial(_proj_dn_kernel, tm=tm, dn_dim=dn_dim),
        grid=(bsz, nl),
        in_specs=[pl.BlockSpec((tm, d), tok),
                  _const_spec(wqkv.shape), _const_spec(wz.shape), _const_spec(wab.shape),
                  _const_spec(conv_w.shape), _const_spec(gpar.shape)],
        out_specs=[pl.BlockSpec((tm, 3 * dn_dim), tok),
                   pl.BlockSpec((tm, dn_dim), tok),
                   pl.BlockSpec((tm, LANES), tok)],
        out_shape=[jax.ShapeDtypeStruct((bsz * seq, 3 * dn_dim), BF16),
                   jax.ShapeDtypeStruct((bsz * seq, dn_dim), BF16),
                   jax.ShapeDtypeStruct((bsz * seq, LANES), F32)],
        scratch_shapes=[pltpu.VMEM((tm + CONV_PAD, 3 * dn_dim), F32)],
        compiler_params=_params(2),
        name="proj_dn",
    )(x2, wqkv, wz, wab, conv_w, gpar)


def _delta_kernel(q_ref, k_ref, v_ref, gb_ref, sz_ref, ng_ref, o_ref, state_ref, *, tc):
    c = DELTA_CHUNK
    hd = DN_HEAD_DIM
    pw = 2 * hd
    assert c == hd

    @pl.when(pl.program_id(1) == 0)
    def _():
        state_ref[...] = jnp.zeros(state_ref.shape, F32)

    row = lax.broadcasted_iota(jnp.int32, (c, pw), 0)
    lane = lax.broadcasted_iota(jnp.int32, (c, pw), 1)
    first = lane < hd
    col = jnp.where(first, lane, lane - hd)
    causal = row >= col
    strict = row > col
    row_s = lax.broadcasted_iota(jnp.int32, (pw, pw), 0)
    lane_s = lax.broadcasted_iota(jnp.int32, (pw, pw), 1)
    same_head = (row_s < hd) == (lane_s < hd)
    trow = lax.broadcasted_iota(jnp.int32, (c, c), 0)
    tcol = lax.broadcasted_iota(jnp.int32, (c, c), 1)
    tri = (trow >= tcol).astype(F32)
    norm_g = jnp.concatenate([ng_ref[...], ng_ref[...]], axis=1)

    def blockdiag(x):
        z = jnp.zeros_like(x)
        return jnp.concatenate([jnp.where(first, x, z), jnp.where(first, z, x)], axis=0)

    def pair_cols(a, j0, j1):
        return jnp.concatenate([jnp.broadcast_to(a[:, j0:j0 + 1], (c, hd)),
                                jnp.broadcast_to(a[:, j1:j1 + 1], (c, hd))], axis=1)

    def pair_rows(a, j0, j1, r):
        return jnp.concatenate([jnp.broadcast_to(a[r:r + 1, j0:j0 + 1], (1, hd)),
                                jnp.broadcast_to(a[r:r + 1, j1:j1 + 1], (1, hd))], axis=1)

    n_pairs = DN_HEADS // 2
    n_chunks = tc // c
    units = [(ci, p) for ci in range(n_chunks) for p in range(n_pairs)]

    gcums = []
    for ci in range(n_chunks):
        gb = gb_ref[ci * c:(ci + 1) * c, :]
        gcum = _dot_f32(tri, gb)
        gcums.append((gb, gcum, gcum.T))
    pre = {}
    for ci, p in units:
        gb, gcum, gcum_t = gcums[ci]
        h0, h1 = 2 * p, 2 * p + 1
        rows = slice(ci * c, (ci + 1) * c)
        cols = slice(p * pw, (p + 1) * pw)
        q = q_ref[rows, cols].astype(F32)
        k_b = k_ref[rows, cols]
        k = k_b.astype(F32)
        v = v_ref[rows, cols].astype(F32)
        g_col = pair_cols(gcum, h0, h1)
        g_row = jnp.concatenate([gcum_t[h0:h0 + 1, :], gcum_t[h1:h1 + 1, :]], axis=1)
        g_last = pair_rows(gcum, h0, h1, c - 1)
        beta = pair_cols(gb, DN_HEADS + h0, DN_HEADS + h1)
        decay = jnp.where(causal, jnp.exp(jnp.where(causal, g_col - g_row, 0.0)), 0.0)
        exp_g = jnp.exp(g_col)
        k_beta = k * beta
        aq = lax.dot_general(jnp.concatenate([k_beta, q], axis=0).astype(BF16), blockdiag(k_b),
                             NT_DIMS, preferred_element_type=F32)
        pre[ci, p] = dict(
            neg_a=jnp.where(strict, -(aq[:c] * decay), 0.0),
            qk=(aq[c:] * decay).astype(BF16),
            vb_bd=blockdiag((v * beta).astype(BF16)),
            kbe_bd=blockdiag((k_beta * exp_g).astype(BF16)),
            q_exp=(q * exp_g).astype(BF16),
            k_dec=(k * jnp.exp(g_last - g_col)).astype(BF16),
            exp_last=jnp.exp(g_last))

    half = c // 2
    row_h = lax.broadcasted_iota(jnp.int32, (half, pw), 0)
    lane_h = lax.broadcasted_iota(jnp.int32, (half, pw), 1)
    blk_id = lax.shift_right_logical(lane_h, half.bit_length() - 1)
    blk = [blk_id == r for r in range(pw // half)]
    left_h = (blk_id & 1) == 0
    eye_h = (row_h == (lane_h & (half - 1))).astype(F32)

    def quad_blockdiag(x):
        z = jnp.zeros_like(x)
        return jnp.concatenate([jnp.where(b, x, z) for b in blk], axis=0)

    def bd_parts(x):
        hi, lo = _split(x)
        return quad_blockdiag(hi), quad_blockdiag(lo)

    nil = {u: jnp.where(left_h, pre[u]["neg_a"][:half], pre[u]["neg_a"][half:]) for u in units}
    t_diag = {u: eye_h + nil[u] for u in units}
    for u in units:
        nil[u], = _dot3_rows([nil[u]], *bd_parts(nil[u]))
    levels = half.bit_length() - 2
    for lvl in range(levels):
        rhs = {u: bd_parts(nil[u]) for u in units}
        for u in units:
            if lvl + 1 < levels:
                t_inc, nil[u] = _dot3_rows([t_diag[u], nil[u]], *rhs[u])
            else:
                t_inc, = _dot3_rows([t_diag[u]], *rhs[u])
            t_diag[u] = t_diag[u] + t_inc
    t_inv = {}
    x21 = {}
    for u in units:
        n21 = jnp.where(left_h, pre[u]["neg_a"][half:], 0.0)
        x21[u], = _dot3_rows([n21], *bd_parts(t_diag[u]))
    for u in units:
        x_hi, x_lo = _split(x21[u])
        z = jnp.zeros_like(x_hi)
        rhs_hi = jnp.concatenate([z, jnp.where(blk[0], x_hi, z), z, jnp.where(blk[2], x_hi, z)], axis=0)
        rhs_lo = jnp.concatenate([z, jnp.where(blk[0], x_lo, z), z, jnp.where(blk[2], x_lo, z)], axis=0)
        t21, = _dot3_rows([t_diag[u]], rhs_hi, rhs_lo)
        t_inv[u] = jnp.concatenate([jnp.where(left_h, t_diag[u], 0.0),
                                    jnp.where(left_h, t21, t_diag[u])], axis=0)
    u_mat, w_mat = {}, {}
    for u in units:
        t_b = t_inv[u].astype(BF16)
        u_mat[u] = jnp.dot(t_b, pre[u]["vb_bd"], preferred_element_type=F32)
        w_mat[u] = jnp.dot(t_b, pre[u]["kbe_bd"], preferred_element_type=F32).astype(BF16)

    states = [state_ref[p] for p in range(n_pairs)]
    for ci in range(n_chunks):
        rows = slice(ci * c, (ci + 1) * c)
        for p in range(n_pairs):
            u = (ci, p)
            cols = slice(p * pw, (p + 1) * pw)
            state = states[p]
            ws_qs = jnp.dot(jnp.concatenate([w_mat[u], pre[u]["q_exp"]], axis=0), state.astype(BF16),
                            preferred_element_type=F32)
            v_new_b = (u_mat[u] - ws_qs[:c]).astype(BF16)
            out = ws_qs[c:] + jnp.dot(pre[u]["qk"], blockdiag(v_new_b), preferred_element_type=F32)
            kv = lax.dot_general(pre[u]["k_dec"], v_new_b, TN_DIMS, preferred_element_type=F32)
            states[p] = jnp.where(same_head, state * pre[u]["exp_last"] + kv, 0.0)
            sq = out * out
            ms = jnp.concatenate(
                [jnp.broadcast_to(jnp.mean(sq[:, :hd], axis=-1, keepdims=True), (c, hd)),
                 jnp.broadcast_to(jnp.mean(sq[:, hd:], axis=-1, keepdims=True), (c, hd))], axis=1)
            out = out * lax.rsqrt(ms + NORM_EPS) * norm_g * sz_ref[rows, cols].astype(F32)
            o_ref[rows, cols] = out.astype(BF16)
    for p in range(n_pairs):
        state_ref[p] = states[p]


def _delta(qkv, sz, gb, norm_g, *, bsz, seq, tc):
    dn_dim = sz.shape[1]
    nl = seq // tc
    tok = lambda b, i: (b * nl + i, 0)
    return pl.pallas_call(
        functools.partial(_delta_kernel, tc=tc),
        grid=(bsz, nl),
        in_specs=[pl.BlockSpec((tc, dn_dim), lambda b, i: (b * nl + i, 0)),
                  pl.BlockSpec((tc, dn_dim), lambda b, i: (b * nl + i, 1)),
                  pl.BlockSpec((tc, dn_dim), lambda b, i: (b * nl + i, 2)),
                  pl.BlockSpec((tc, LANES), tok),
                  pl.BlockSpec((tc, dn_dim), tok),
                  _const_spec(norm_g.shape)],
        out_specs=pl.BlockSpec((tc, dn_dim), tok),
        out_shape=jax.ShapeDtypeStruct((bsz * seq, dn_dim), BF16),
        scratch_shapes=[pltpu.VMEM((DN_HEADS // 2, 2 * DN_HEAD_DIM, 2 * DN_HEAD_DIM), F32)],
        compiler_params=_params(2),
        name="delta",
    )(qkv, qkv, qkv, gb, sz, norm_g)


def _sgu_kernel(x_ref, wuv_ref, wgate_ref, lng_ref, lnb_ref, sw_ref, sb_ref, wso_ref,
                ys_ref, sgd_ref, u_ref, v_ref, gated_ref, *, tm, sgu_dim, d_model):
    xb = x_ref[...].astype(BF16)
    inv_sqrt2 = 2.0 ** -0.5
    for n in range(0, 2 * sgu_dim, 256):
        uv = jnp.dot(xb, wuv_ref[:, n:n + 256], preferred_element_type=F32)
        act = 0.5 * uv * (1.0 + lax.erf(uv * inv_sqrt2))
        if n < sgu_dim:
            u_ref[:, n:n + 256] = act
        else:
            v_ref[:, n - sgu_dim:n - sgu_dim + 256] = act

    v = v_ref[...]
    vc = v - jnp.mean(v, axis=-1, keepdims=True)
    var = jnp.mean(vc * vc, axis=-1, keepdims=True)
    v_ref[...] = vc * lax.rsqrt(var + LN_EPS) * lng_ref[...] + lnb_ref[...]

    row = lax.broadcasted_iota(jnp.int32, (SGU_CHUNK, SGU_CHUNK), 0)
    col = lax.broadcasted_iota(jnp.int32, (SGU_CHUNK, SGU_CHUNK), 1)
    causal = row >= col
    for g in range(SGU_GROUPS):
        cols = slice(g * LANES, (g + 1) * LANES)
        w_causal = jnp.where(causal, sw_ref[g], 0.0).astype(BF16)
        bias = sb_ref[:, g:g + 1]
        for ci in range(tm // SGU_CHUNK):
            rows = slice(ci * SGU_CHUNK, (ci + 1) * SGU_CHUNK)
            mixed = jnp.dot(w_causal, v_ref[rows, cols].astype(BF16), preferred_element_type=F32) + bias
            gated_ref[rows, cols] = (u_ref[rows, cols] * mixed).astype(BF16)

    gated = gated_ref[...]
    for n in range(0, d_model, 256):
        y_sgu = jnp.dot(gated, wso_ref[:, n:n + 256], preferred_element_type=F32)
        gate_sgu = jnp.dot(xb, wgate_ref[:, d_model + n:d_model + n + 256], preferred_element_type=F32)
        ys_ref[:, n:n + 256] = (_sigmoid(gate_sgu) * y_sgu).astype(BF16)
        gate_dn = jnp.dot(xb, wgate_ref[:, n:n + 256], preferred_element_type=F32)
        sgd_ref[:, n:n + 256] = _sigmoid(gate_dn).astype(BF16)


def _sgu(x2, wuv, wgate, ln_g, ln_b, spatial_w, spatial_b_t, wso, *, tm):
    t, d = x2.shape
    sgu_dim = wso.shape[0]
    tok = lambda i: (i, 0)
    return pl.pallas_call(
        functools.partial(_sgu_kernel, tm=tm, sgu_dim=sgu_dim, d_model=d),
        grid=(t // tm,),
        in_specs=[pl.BlockSpec((tm, d), tok),
                  _const_spec(wuv.shape), _const_spec(wgate.shape),
                  _const_spec(ln_g.shape), _const_spec(ln_b.shape),
                  _const_spec(spatial_w.shape), _const_spec(spatial_b_t.shape),
                  _const_spec(wso.shape)],
        out_specs=[pl.BlockSpec((tm, d), tok), pl.BlockSpec((tm, d), tok)],
        out_shape=[jax.ShapeDtypeStruct((t, d), BF16), jax.ShapeDtypeStruct((t, d), BF16)],
        scratch_shapes=[pltpu.VMEM((tm, sgu_dim), F32), pltpu.VMEM((tm, sgu_dim), F32),
                        pltpu.VMEM((tm, sgu_dim), BF16)],
        compiler_params=_params(1),
        name="sgu",
    )(x2, wuv, wgate, ln_g, ln_b, spatial_w, spatial_b_t, wso)


def _layer_norm(r, g, b):
    rc = r - jnp.mean(r, axis=-1, keepdims=True)
    var = jnp.mean(rc * rc, axis=-1, keepdims=True)
    return rc * lax.rsqrt(var + LN_EPS) * g + b


def _merge_kernel(x_ref, o_ref, sgd_ref, ys_ref, wdn_ref, wout_ref, g_ref, b_ref, h_ref, *, alpha):
    y_dn = jnp.dot(o_ref[...], wdn_ref[...], preferred_element_type=F32)
    y = sgd_ref[...].astype(F32) * y_dn + ys_ref[...].astype(F32)
    mix = jnp.dot(y.astype(BF16), wout_ref[...], preferred_element_type=F32)
    h_ref[...] = _layer_norm(alpha * x_ref[...] + mix, g_ref[...], b_ref[...])


def _merge(x2, o, sgd, ys, wdn, wout, ln_g, ln_b, *, tm, alpha):
    t, d = x2.shape
    tok = lambda i: (i, 0)
    return pl.pallas_call(
        functools.partial(_merge_kernel, alpha=alpha),
        grid=(t // tm,),
        in_specs=[pl.BlockSpec((tm, d), tok), pl.BlockSpec((tm, o.shape[1]), tok),
                  pl.BlockSpec((tm, d), tok), pl.BlockSpec((tm, d), tok),
                  _const_spec(wdn.shape), _const_spec(wout.shape),
                  _const_spec(ln_g.shape), _const_spec(ln_b.shape)],
        out_specs=pl.BlockSpec((tm, d), tok),
        out_shape=jax.ShapeDtypeStruct((t, d), F32),
        compiler_params=_params(1),
        name="merge",
    )(x2, o, sgd, ys, wdn, wout, ln_g, ln_b)


def _moe_kernel(h_ref, wr_ref, br_ref, wg_ref, wu_ref, wd_ref, g_ref, b_ref, out_ref, hm_ref,
                *, alpha, expert_ff):
    h = h_ref[...]
    hb = h.astype(BF16)
    neg_inf = -jnp.inf
    logits = _dot_f32(h, wr_ref[...]) + br_ref[...]
    lane = lax.broadcasted_iota(jnp.int32, logits.shape, 1)
    big = jnp.int32(LANES)

    g_logit = jnp.where(lane < N_GROUPS, logits, neg_inf)
    g_max = jnp.max(g_logit, axis=-1, keepdims=True)
    g_idx = jnp.min(jnp.where(g_logit == g_max, lane, big), axis=-1, keepdims=True)
    group_p = 1.0 / jnp.sum(jnp.exp(g_logit - g_max), axis=-1, keepdims=True)

    lo = N_GROUPS + EXPERTS_PER_GROUP * g_idx
    e_logit = jnp.where(lane >= lo, jnp.where(lane < lo + EXPERTS_PER_GROUP, logits, neg_inf), neg_inf)
    e1 = jnp.max(e_logit, axis=-1, keepdims=True)
    i1 = jnp.min(jnp.where(e_logit == e1, lane, big), axis=-1, keepdims=True)
    e_rest = jnp.where(lane == i1, neg_inf, e_logit)
    e2 = jnp.max(e_rest, axis=-1, keepdims=True)
    i2 = jnp.min(jnp.where(e_rest == e2, lane, big), axis=-1, keepdims=True)
    r = jnp.exp(e2 - e1)
    p1 = group_p / (1.0 + r)
    p2 = group_p * r / (1.0 + r)
    combine = jnp.where(lane == i1, p1, jnp.where(lane == i2, p2, 0.0))

    for e in range(N_EXPERTS):
        hg = jnp.dot(hb, wg_ref[e], preferred_element_type=F32)
        hu = jnp.dot(hb, wu_ref[e], preferred_element_type=F32)
        c_e = combine[:, N_GROUPS + e:N_GROUPS + e + 1]
        hm_ref[:, e * expert_ff:(e + 1) * expert_ff] = (_silu(hg) * hu * c_e).astype(BF16)

    ffn = jnp.dot(hm_ref[...], wd_ref[...], preferred_element_type=F32)
    out_ref[...] = _layer_norm(alpha * h + ffn, g_ref[...], b_ref[...])


def _moe(h1, wr, br, wg, wu, wd, ln_g, ln_b, *, tm, alpha):
    t, d = h1.shape
    expert_ff = wg.shape[2]
    tok = lambda i: (i, 0)
    return pl.pallas_call(
        functools.partial(_moe_kernel, alpha=alpha, expert_ff=expert_ff),
        grid=(t // tm,),
        in_specs=[pl.BlockSpec((tm, d), tok),
                  _const_spec(wr.shape), _const_spec(br.shape),
                  _const_spec(wg.shape), _const_spec(wu.shape), _const_spec(wd.shape),
                  _const_spec(ln_g.shape), _const_spec(ln_b.shape)],
        out_specs=pl.BlockSpec((tm, d), tok),
        out_shape=jax.ShapeDtypeStruct((t, d), F32),
        scratch_shapes=[pltpu.VMEM((tm, N_EXPERTS * expert_ff), BF16)],
        compiler_params=_params(1),
        name="moe",
    )(h1, wr, br, wg, wu, wd, ln_g, ln_b)


def _pad_lanes(a, width=LANES):
    return jnp.pad(a, [(0, 0)] * (a.ndim - 1) + [(0, width - a.shape[-1])])


def _layer(h, w_in, conv_w, a_log, dt_bias, dn_norm_g, w_dn_out, sgu_ln_g, sgu_ln_b,
           spatial_w, spatial_b, w_sgu_out, w_out, ln1_g, ln1_b,
           router_group_w, router_group_b, router_expert_w, router_expert_b,
           expert_w_gate, expert_w_up, expert_w_down, ln2_g, ln2_b, *, alpha):
    bsz, seq, d = h.shape
    dn_dim = w_dn_out.shape[0]
    sgu_dim = w_sgu_out.shape[0]
    x2 = h.reshape(bsz * seq, d)

    o_z = 3 * dn_dim
    o_a = o_z + dn_dim
    o_uv = o_a + 2 * DN_HEADS
    o_gate = o_uv + 2 * sgu_dim
    wqkv = w_in[:, :o_z].astype(BF16)
    wz = w_in[:, o_z:o_a].astype(BF16)
    wab = _pad_lanes(w_in[:, o_a:o_uv]).astype(BF16)
    wuv = w_in[:, o_uv:o_gate].astype(BF16)
    wgate = w_in[:, o_gate:].astype(BF16)
    gpar = jnp.stack([_pad_lanes(-jnp.exp(a_log.astype(F32))), _pad_lanes(dt_bias.astype(F32))])

    qkv, sz, gb = _proj_dn(x2, wqkv, wz, wab, conv_w, gpar, bsz=bsz, seq=seq, tm=256)
    o = _delta(qkv, sz, gb, dn_norm_g.reshape(1, -1), bsz=bsz, seq=seq, tc=256)
    ys, sgd = _sgu(x2, wuv, wgate, sgu_ln_g.reshape(1, -1), sgu_ln_b.reshape(1, -1),
                   spatial_w, spatial_b.T, w_sgu_out.astype(BF16), tm=256)
    h1 = _merge(x2, o, sgd, ys, w_dn_out.astype(BF16), w_out.astype(BF16),
                ln1_g.reshape(1, -1), ln1_b.reshape(1, -1), tm=512, alpha=alpha)

    wr = _pad_lanes(jnp.concatenate([router_group_w, router_expert_w], axis=1))
    br = _pad_lanes(jnp.concatenate([router_group_b, router_expert_b]).reshape(1, -1))
    wd = expert_w_down.astype(BF16).reshape(-1, d)
    h2 = _moe(h1, wr, br, expert_w_gate.astype(BF16), expert_w_up.astype(BF16), wd,
              ln2_g.reshape(1, -1), ln2_b.reshape(1, -1), tm=256, alpha=alpha)
    return h2.reshape(bsz, seq, d)


def kernel(x, w_in, conv_w, a_log, dt_bias, dn_norm_g, w_dn_out, sgu_ln_g, sgu_ln_b, spatial_w, spatial_b, w_sgu_out, w_out, ln1_g, ln1_b, router_group_w, router_group_b, router_expert_w, router_expert_b, expert_w_gate, expert_w_up, expert_w_down, ln2_g, ln2_b):
    depth = w_in.shape[0]
    alpha = (2.0 * depth) ** 0.25
    h = x
    for l in range(depth):
        h = _layer(h, w_in[l], conv_w[l], a_log[l], dt_bias[l], dn_norm_g[l], w_dn_out[l],
                   sgu_ln_g[l], sgu_ln_b[l], spatial_w[l], spatial_b[l], w_sgu_out[l], w_out[l],
                   ln1_g[l], ln1_b[l], router_group_w[l], router_group_b[l],
                   router_expert_w[l], router_expert_b[l],
                   expert_w_gate[l], expert_w_up[l], expert_w_down[l], ln2_g[l], ln2_b[l], alpha=alpha)
    return h
```

```python
import functools

import jax
import jax.numpy as jnp
from jax import lax
from jax.experimental import pallas as pl
from jax.experimental.pallas import tpu as pltpu

F32 = jnp.float32
BF16 = jnp.bfloat16

LANES = 128
DN_HEADS = 8
DN_HEAD_DIM = 128
CONV_WIDTH = 4
SGU_GROUPS = 8
SGU_CHUNK = 128
N_GROUPS = 4
EXPERTS_PER_GROUP = 4
N_EXPERTS = N_GROUPS * EXPERTS_PER_GROUP
LN_EPS = 1e-5
NORM_EPS = 1e-6
DELTA_CHUNK = 128
CONV_PAD = 8
MOE_BLOCK = 512
SUB_ROWS = 160
VMEM_LIMIT = 56 * 1024 * 1024

NT_DIMS = (((1,), (1,)), ((), ()))
TN_DIMS = (((0,), (0,)), ((), ()))


def _dot_f32(a, b):
    return jnp.dot(a, b, preferred_element_type=F32, precision=lax.Precision.HIGHEST)


def _split(x):
    hi = x.astype(BF16)
    return hi, (x - hi.astype(F32)).astype(BF16)


def _dot3_rows(lhs, b_hi, b_lo):
    m = lhs[0].shape[0]
    n = len(lhs)
    parts = [_split(a) for a in lhs]
    his = [hi for hi, _ in parts]
    los = [lo for _, lo in parts]
    r_hi = jnp.dot(jnp.concatenate(his + los, axis=0), b_hi, preferred_element_type=F32)
    r_lo = jnp.dot(jnp.concatenate(his, axis=0) if n > 1 else his[0], b_lo, preferred_element_type=F32)
    return [r_hi[i * m:(i + 1) * m] + r_hi[(n + i) * m:(n + i + 1) * m] + r_lo[i * m:(i + 1) * m]
            for i in range(n)]


def _sigmoid(x):
    return 1.0 / (1.0 + jnp.exp(-x))


def _silu(x):
    return x * _sigmoid(x)


def _const_spec(shape):
    nd = len(shape)
    return pl.BlockSpec(shape, lambda *_: (0,) * nd, pipeline_mode=pl.Buffered(1))


def _params(n_axes):
    return pltpu.CompilerParams(dimension_semantics=("arbitrary",) * n_axes,
                                vmem_limit_bytes=VMEM_LIMIT)


def _proj_dn_kernel(x_ref, wqkv_ref, wz_ref, wab_ref, conv_ref, gpar_ref,
                    qkv_ref, sz_ref, gb_ref, ext_ref, *, tm, dn_dim):
    i = pl.program_id(1)
    xb = x_ref[...].astype(BF16)
    n_qkv = 3 * dn_dim

    @pl.when(i == 0)
    def _():
        ext_ref[0:CONV_PAD, :] = jnp.zeros((CONV_PAD, n_qkv), F32)

    for n in range(0, n_qkv, 256):
        ext_ref[CONV_PAD:CONV_PAD + tm, n:n + 256] = jnp.dot(
            xb, wqkv_ref[:, n:n + 256], preferred_element_type=F32)

    for n in range(0, n_qkv, LANES):
        w = conv_ref[:, n:n + LANES]
        y = w[3:4, :] * ext_ref[CONV_PAD:CONV_PAD + tm, n:n + LANES]
        for j in range(1, CONV_WIDTH):
            y = y + w[3 - j:4 - j, :] * ext_ref[CONV_PAD - j:CONV_PAD - j + tm, n:n + LANES]
        y = _silu(y)
        if n < 2 * dn_dim:
            y = y * lax.rsqrt(jnp.sum(y * y, axis=-1, keepdims=True) + NORM_EPS)
            if n < dn_dim:
                y = y * (DN_HEAD_DIM ** -0.5)
        qkv_ref[:, n:n + LANES] = y.astype(BF16)

    ext_ref[0:CONV_PAD, :] = ext_ref[tm:tm + CONV_PAD, :]

    for n in range(0, dn_dim, 256):
        z = jnp.dot(xb, wz_ref[:, n:n + 256], preferred_element_type=F32)
        sz_ref[:, n:n + 256] = _silu(z).astype(BF16)

    ab = jnp.dot(xb, wab_ref[...], preferred_element_type=F32)
    neg_exp_alog = gpar_ref[0:1, :]
    dt_bias = gpar_ref[1:2, :]
    sp_in = ab + dt_bias
    softplus = jnp.maximum(sp_in, 0.0) + jnp.log1p(jnp.exp(-jnp.abs(sp_in)))
    lane = lax.broadcasted_iota(jnp.int32, ab.shape, 1)
    gb_ref[...] = jnp.where(lane < DN_HEADS, neg_exp_alog * softplus, _sigmoid(ab))


def _proj_dn(x2, wqkv, wz, wab, conv_w, gpar, *, bsz, seq, tm):
    d = x2.shape[1]
    dn_dim = wz.shape[1]
    nl = seq // tm
    tok = lambda b, i: (b * nl + i, 0)
    return pl.pallas_call(
        functools.partial(_proj_dn_kernel, tm=tm, dn_dim=dn_dim),
        grid=(bsz, nl),
        in_specs=[pl.BlockSpec((tm, d), tok),
                  _const_spec(wqkv.shape), _const_spec(wz.shape), _const_spec(wab.shape),
                  _const_spec(conv_w.shape), _const_spec(gpar.shape)],
        out_specs=[pl.BlockSpec((tm, 3 * dn_dim), tok),
                   pl.BlockSpec((tm, dn_dim), tok),
                   pl.BlockSpec((tm, LANES), tok)],
        out_shape=[jax.ShapeDtypeStruct((bsz * seq, 3 * dn_dim), BF16),
                   jax.ShapeDtypeStruct((bsz * seq, dn_dim), BF16),
                   jax.ShapeDtypeStruct((bsz * seq, LANES), F32)],
        scratch_shapes=[pltpu.VMEM((tm + CONV_PAD, 3 * dn_dim), F32)],
        compiler_params=_params(2),
        name="proj_dn",
    )(x2, wqkv, wz, wab, conv_w, gpar)


def _delta_kernel(q_ref, k_ref, v_ref, gb_ref, sz_ref, ng_ref, o_ref, state_ref, *, tc):
    c = DELTA_CHUNK
    hd = DN_HEAD_DIM
    pw = 2 * hd
    assert c == hd

    @pl.when(pl.program_id(1) == 0)
    def _():
        state_ref[...] = jnp.zeros(state_ref.shape, F32)

    row = lax.broadcasted_iota(jnp.int32, (c, pw), 0)
    lane = lax.broadcasted_iota(jnp.int32, (c, pw), 1)
    first = lane < hd
    col = jnp.where(first, lane, lane - hd)
    causal = row >= col
    strict = row > col
    row_s = lax.broadcasted_iota(jnp.int32, (pw, pw), 0)
    lane_s = lax.broadcasted_iota(jnp.int32, (pw, pw), 1)
    same_head = (row_s < hd) == (lane_s < hd)
    trow = lax.broadcasted_iota(jnp.int32, (c, c), 0)
    tcol = lax.broadcasted_iota(jnp.int32, (c, c), 1)
    tri = (trow >= tcol).astype(F32)
    norm_g = jnp.concatenate([ng_ref[...], ng_ref[...]], axis=1)

    def blockdiag(x):
        z = jnp.zeros_like(x)
        return jnp.concatenate([jnp.where(first, x, z), jnp.where(first, z, x)], axis=0)

    def pair_cols(a, j0, j1):
        return jnp.concatenate([jnp.broadcast_to(a[:, j0:j0 + 1], (c, hd)),
                                jnp.broadcast_to(a[:, j1:j1 + 1], (c, hd))], axis=1)

    def pair_rows(a, j0, j1, r):
        return jnp.concatenate([jnp.broadcast_to(a[r:r + 1, j0:j0 + 1], (1, hd)),
                                jnp.broadcast_to(a[r:r + 1, j1:j1 + 1], (1, hd))], axis=1)

    n_pairs = DN_HEADS // 2
    n_chunks = tc // c
    units = [(ci, p) for ci in range(n_chunks) for p in range(n_pairs)]

    gcums = []
    for ci in range(n_chunks):
        gb = gb_ref[ci * c:(ci + 1) * c, :]
        gcum = _dot_f32(tri, gb)
        gcums.append((gb, gcum, gcum.T))
    pre = {}
    for ci, p in units:
        gb, gcum, gcum_t = gcums[ci]
        h0, h1 = 2 * p, 2 * p + 1
        rows = slice(ci * c, (ci + 1) * c)
        cols = slice(p * pw, (p + 1) * pw)
        q = q_ref[rows, cols].astype(F32)
        k_b = k_ref[rows, cols]
        k = k_b.astype(F32)
        v = v_ref[rows, cols].astype(F32)
        g_col = pair_cols(gcum, h0, h1)
        g_row = jnp.concatenate([gcum_t[h0:h0 + 1, :], gcum_t[h1:h1 + 1, :]], axis=1)
        g_last = pair_rows(gcum, h0, h1, c - 1)
        beta = pair_cols(gb, DN_HEADS + h0, DN_HEADS + h1)
        decay = jnp.where(causal, jnp.exp(jnp.where(causal, g_col - g_row, 0.0)), 0.0)
        exp_g = jnp.exp(g_col)
        k_beta = k * beta
        aq = lax.dot_general(jnp.concatenate([k_beta, q], axis=0).astype(BF16), blockdiag(k_b),
                             NT_DIMS, preferred_element_type=F32)
        pre[ci, p] = dict(
            neg_a=jnp.where(strict, -(aq[:c] * decay), 0.0),
            qk=(aq[c:] * decay).astype(BF16),
            vb_bd=blockdiag((v * beta).astype(BF16)),
            kbe_bd=blockdiag((k_beta * exp_g).astype(BF16)),
            q_exp=(q * exp_g).astype(BF16),
            k_dec=(k * jnp.exp(g_last - g_col)).astype(BF16),
            exp_last=jnp.exp(g_last))

    half = c // 2
    row_h = lax.broadcasted_iota(jnp.int32, (half, pw), 0)
    lane_h = lax.broadcasted_iota(jnp.int32, (half, pw), 1)
    blk_id = lax.shift_right_logical(lane_h, half.bit_length() - 1)
    blk = [blk_id == r for r in range(pw // half)]
    left_h = (blk_id & 1) == 0
    eye_h = (row_h == (lane_h & (half - 1))).astype(F32)

    def quad_blockdiag(x):
        z = jnp.zeros_like(x)
        return jnp.concatenate([jnp.where(b, x, z) for b in blk], axis=0)

    def bd_parts(x):
        hi, lo = _split(x)
        return quad_blockdiag(hi), quad_blockdiag(lo)

    nil = {u: jnp.where(left_h, pre[u]["neg_a"][:half], pre[u]["neg_a"][half:]) for u in units}
    t_diag = {u: eye_h + nil[u] for u in units}
    for u in units:
        nil[u], = _dot3_rows([nil[u]], *bd_parts(nil[u]))
    levels = half.bit_length() - 2
    for lvl in range(levels):
        rhs = {u: bd_parts(nil[u]) for u in units}
        for u in units:
            if lvl + 1 < levels:
                t_inc, nil[u] = _dot3_rows([t_diag[u], nil[u]], *rhs[u])
            else:
                t_inc, = _dot3_rows([t_diag[u]], *rhs[u])
            t_diag[u] = t_diag[u] + t_inc
    t_inv = {}
    x21 = {}
    for u in units:
        n21 = jnp.where(left_h, pre[u]["neg_a"][half:], 0.0)
        x21[u], = _dot3_rows([n21], *bd_parts(t_diag[u]))
    for u in units:
        x_hi, x_lo = _split(x21[u])
        z = jnp.zeros_like(x_hi)
        rhs_hi = jnp.concatenate([z, jnp.where(blk[0], x_hi, z), z, jnp.where(blk[2], x_hi, z)], axis=0)
        rhs_lo = jnp.concatenate([z, jnp.where(blk[0], x_lo, z), z, jnp.where(blk[2], x_lo, z)], axis=0)
        t21, = _dot3_rows([t_diag[u]], rhs_hi, rhs_lo)
        t_inv[u] = jnp.concatenate([jnp.where(left_h, t_diag[u], 0.0),
                                    jnp.where(left_h, t21, t_diag[u])], axis=0)
    u_mat, w_mat = {}, {}
    for u in units:
        t_b = t_inv[u].astype(BF16)
        u_mat[u] = jnp.dot(t_b, pre[u]["vb_bd"], preferred_element_type=F32)
        w_mat[u] = jnp.dot(t_b, pre[u]["kbe_bd"], preferred_element_type=F32).astype(BF16)

    states = [state_ref[p] for p in range(n_pairs)]
    for ci in range(n_chunks):
        rows = slice(ci * c, (ci + 1) * c)
        for p in range(n_pairs):
            u = (ci, p)
            cols = slice(p * pw, (p + 1) * pw)
            state = states[p]
            ws_qs = jnp.dot(jnp.concatenate([w_mat[u], pre[u]["q_exp"]], axis=0), state.astype(BF16),
                            preferred_element_type=F32)
            v_new_b = (u_mat[u] - ws_qs[:c]).astype(BF16)
            out = ws_qs[c:] + jnp.dot(pre[u]["qk"], blockdiag(v_new_b), preferred_element_type=F32)
            kv = lax.dot_general(pre[u]["k_dec"], v_new_b, TN_DIMS, preferred_element_type=F32)
            states[p] = jnp.where(same_head, state * pre[u]["exp_last"] + kv, 0.0)
            sq = out * out
            ms = jnp.concatenate(
                [jnp.broadcast_to(jnp.mean(sq[:, :hd], axis=-1, keepdims=True), (c, hd)),
                 jnp.broadcast_to(jnp.mean(sq[:, hd:], axis=-1, keepdims=True), (c, hd))], axis=1)
            out = out * lax.rsqrt(ms + NORM_EPS) * norm_g * sz_ref[rows, cols].astype(F32)
            o_ref[rows, cols] = out.astype(BF16)
    for p in range(n_pairs):
        state_ref[p] = states[p]


def _delta(qkv, sz, gb, norm_g, *, bsz, seq, tc):
    dn_dim = sz.shape[1]
    nl = seq // tc
    tok = lambda b, i: (b * nl + i, 0)
    return pl.pallas_call(
        functools.partial(_delta_kernel, tc=tc),
        grid=(bsz, nl),
        in_specs=[pl.BlockSpec((tc, dn_dim), lambda b, i: (b * nl + i, 0)),
                  pl.BlockSpec((tc, dn_dim), lambda b, i: (b * nl + i, 1)),
                  pl.BlockSpec((tc, dn_dim), lambda b, i: (b * nl + i, 2)),
                  pl.BlockSpec((tc, LANES), tok),
                  pl.BlockSpec((tc, dn_dim), tok),
                  _const_spec(norm_g.shape)],
        out_specs=pl.BlockSpec((tc, dn_dim), tok),
        out_shape=jax.ShapeDtypeStruct((bsz * seq, dn_dim), BF16),
        scratch_shapes=[pltpu.VMEM((DN_HEADS // 2, 2 * DN_HEAD_DIM, 2 * DN_HEAD_DIM), F32)],
        compiler_params=_params(2),
        name="delta",
    )(qkv, qkv, qkv, gb, sz, norm_g)


def _sgu_body(x_ref, wuv_ref, wgate_ref, lng_ref, lnb_ref, sw_ref, sb_ref, wso_ref, u_ref, v_ref, gated_ref,
              *, tm, sgu_dim, d_model):
    xb = x_ref[...].astype(BF16)
    inv_sqrt2 = 2.0 ** -0.5
    for n in range(0, 2 * sgu_dim, 256):
        uv = jnp.dot(xb, wuv_ref[:, n:n + 256], preferred_element_type=F32)
        act = 0.5 * uv * (1.0 + lax.erf(uv * inv_sqrt2))
        if n < sgu_dim:
            u_ref[:, n:n + 256] = act
        else:
            v_ref[:, n - sgu_dim:n - sgu_dim + 256] = act

    v = v_ref[...]
    vc = v - jnp.mean(v, axis=-1, keepdims=True)
    var = jnp.mean(vc * vc, axis=-1, keepdims=True)
    v_ref[...] = vc * lax.rsqrt(var + LN_EPS) * lng_ref[...] + lnb_ref[...]

    row = lax.broadcasted_iota(jnp.int32, (SGU_CHUNK, SGU_CHUNK), 0)
    col = lax.broadcasted_iota(jnp.int32, (SGU_CHUNK, SGU_CHUNK), 1)
    causal = row >= col
    for g in range(SGU_GROUPS):
        cols = slice(g * LANES, (g + 1) * LANES)
        w_causal = jnp.where(causal, sw_ref[g], 0.0).astype(BF16)
        bias = sb_ref[:, g:g + 1]
        for ci in range(tm // SGU_CHUNK):
            rows = slice(ci * SGU_CHUNK, (ci + 1) * SGU_CHUNK)
            mixed = jnp.dot(w_causal, v_ref[rows, cols].astype(BF16), preferred_element_type=F32) + bias
            gated_ref[rows, cols] = (u_ref[rows, cols] * mixed).astype(BF16)

    gated = gated_ref[...]
    ys, sgd = [], []
    for n in range(0, d_model, 256):
        y_sgu = jnp.dot(gated, wso_ref[:, n:n + 256], preferred_element_type=F32)
        gate_sgu = jnp.dot(xb, wgate_ref[:, d_model + n:d_model + n + 256], preferred_element_type=F32)
        ys.append(_sigmoid(gate_sgu) * y_sgu)
        gate_dn = jnp.dot(xb, wgate_ref[:, n:n + 256], preferred_element_type=F32)
        sgd.append(_sigmoid(gate_dn))
    return jnp.concatenate(ys, axis=1), jnp.concatenate(sgd, axis=1)


def _layer_norm(r, g, b):
    rc = r - jnp.mean(r, axis=-1, keepdims=True)
    var = jnp.mean(rc * rc, axis=-1, keepdims=True)
    return rc * lax.rsqrt(var + LN_EPS) * g + b


def _merge_kernel(x_ref, o_ref, wuv_ref, wgate_ref, lng_ref, lnb_ref, sw_ref, sb_ref, wso_ref,
                  wdn_ref, wout_ref, g_ref, b_ref, wr_ref, br_ref,
                  h_ref, rinfo_ref, dest_ref, nsub_ref, info_ref, u_ref, v_ref, gated_ref,
                  *, alpha, tm, sgu_dim, d_model):
    ys, sgd = _sgu_body(x_ref, wuv_ref, wgate_ref, lng_ref, lnb_ref, sw_ref, sb_ref, wso_ref,
                        u_ref, v_ref, gated_ref, tm=tm, sgu_dim=sgu_dim, d_model=d_model)
    y_dn = jnp.dot(o_ref[...], wdn_ref[...], preferred_element_type=F32)
    y = sgd * y_dn + ys
    mix = jnp.dot(y.astype(BF16), wout_ref[...], preferred_element_type=F32)
    h = _layer_norm(alpha * x_ref[...] + mix, g_ref[...], b_ref[...])
    h_ref[...] = h

    h_hi, h_lo = _split(h)
    parts = jnp.dot(jnp.concatenate([h_hi, h_lo], axis=0), wr_ref[...], preferred_element_type=F32)
    logits = (parts[:tm, :LANES] + parts[tm:, :LANES]) + (parts[:tm, LANES:] + parts[tm:, LANES:]) + br_ref[...]
    lt = logits.T
    row_of = lambda i: lt[i:i + 1, :]
    g_logit = [row_of(g) for g in range(N_GROUPS)]
    g_max = functools.reduce(jnp.maximum, g_logit)
    g_idx = jnp.full(g_max.shape, N_GROUPS - 1, jnp.int32)
    for g in range(N_GROUPS - 2, -1, -1):
        g_idx = jnp.where(g_logit[g] == g_max, g, g_idx)
    group_p = 1.0 / functools.reduce(jnp.add, [jnp.exp(gl - g_max) for gl in g_logit])
    e_logit = []
    for j in range(EXPERTS_PER_GROUP):
        sel = row_of(N_GROUPS + (N_GROUPS - 1) * EXPERTS_PER_GROUP + j)
        for g in range(N_GROUPS - 2, -1, -1):
            sel = jnp.where(g_idx == g, row_of(N_GROUPS + g * EXPERTS_PER_GROUP + j), sel)
        e_logit.append(sel)
    e1 = functools.reduce(jnp.maximum, e_logit)
    j1 = jnp.full(e1.shape, EXPERTS_PER_GROUP - 1, jnp.int32)
    for j in range(EXPERTS_PER_GROUP - 2, -1, -1):
        j1 = jnp.where(e_logit[j] == e1, j, j1)
    rest = [jnp.where(j1 == j, -jnp.inf, e_logit[j]) for j in range(EXPERTS_PER_GROUP)]
    e2 = functools.reduce(jnp.maximum, rest)
    j2 = jnp.full(e2.shape, EXPERTS_PER_GROUP - 1, jnp.int32)
    for j in range(EXPERTS_PER_GROUP - 2, -1, -1):
        j2 = jnp.where(rest[j] == e2, j, j2)
    r = jnp.exp(e2 - e1)
    p1 = group_p / (1.0 + r)
    p2 = group_p * r / (1.0 + r)
    p1_hi = p1.astype(BF16).astype(F32)
    p2_hi = p2.astype(BF16).astype(F32)

    onehot = [jnp.where(g_idx == g, 1.0, 0.0) for g in range(N_GROUPS)]
    trow = lax.broadcasted_iota(jnp.int32, (tm, tm), 0)
    tcol = lax.broadcasted_iota(jnp.int32, (tm, tm), 1)
    earlier = jnp.where(trow < tcol, 1.0, 0.0).astype(BF16)
    onehot_rows = jnp.concatenate(onehot + [jnp.zeros((8 - N_GROUPS, tm), F32)], axis=0).astype(BF16)
    rank = jnp.dot(onehot_rows, earlier, preferred_element_type=F32)
    lane1 = lax.broadcasted_iota(jnp.int32, (1, LANES), 1)
    nsub_row = jnp.zeros((1, LANES), F32)
    dest_row = jnp.zeros((1, tm), F32)
    start = jnp.zeros((1, 1), F32)
    for g in range(N_GROUPS):
        count = jnp.sum(onehot[g], axis=-1, keepdims=True)
        n_sub = jnp.floor((count + (SUB_ROWS - 1)) * (1.0 / SUB_ROWS))
        dest_row = dest_row + onehot[g] * (start * SUB_ROWS + rank[g:g + 1, :])
        nsub_row = jnp.where(lane1 == g, n_sub, jnp.where(lane1 == N_GROUPS + g, start, nsub_row))
        start = start + n_sub
    dest_ref[0] = dest_row.astype(jnp.int32)
    nsub_ref[0] = nsub_row.astype(jnp.int32)
    info_ref[...] = jnp.zeros(info_ref.shape, F32)
    for j in range(EXPERTS_PER_GROUP):
        info_ref[j:j + 1, :] = jnp.where(j1 == j, p1_hi, jnp.where(j2 == j, p2_hi, 0.0))
        info_ref[EXPERTS_PER_GROUP + j:EXPERTS_PER_GROUP + j + 1, :] = jnp.where(
            j1 == j, p1 - p1_hi, jnp.where(j2 == j, p2 - p2_hi, 0.0))
    info_ref[2 * EXPERTS_PER_GROUP:2 * EXPERTS_PER_GROUP + 1, :] = dest_row
    rinfo_ref[...] = info_ref[...].T


def _merge(x2, o, wuv, wgate, sgu_ln_g, sgu_ln_b, spatial_w, spatial_b_t, wso, wdn, wout, ln_g, ln_b, wr, br,
           *, tm, alpha):
    t, d = x2.shape
    sgu_dim = wso.shape[0]
    nb = t // tm
    tok = lambda i: (i, 0)
    consts = (wuv, wgate, sgu_ln_g, sgu_ln_b, spatial_w, spatial_b_t, wso, wdn, wout, ln_g, ln_b, wr, br)
    return pl.pallas_call(
        functools.partial(_merge_kernel, alpha=alpha, tm=tm, sgu_dim=sgu_dim, d_model=d),
        grid=(nb,),
        in_specs=[pl.BlockSpec((tm, d), tok), pl.BlockSpec((tm, o.shape[1]), tok)]
        + [_const_spec(c.shape) for c in consts],
        out_specs=[pl.BlockSpec((tm, d), tok), pl.BlockSpec((tm, LANES), tok),
                   pl.BlockSpec((1, 1, tm), lambda i: (i, 0, 0)),
                   pl.BlockSpec((1, 1, LANES), lambda i: (i, 0, 0))],
        out_shape=[jax.ShapeDtypeStruct((t, d), F32), jax.ShapeDtypeStruct((t, LANES), F32),
                   jax.ShapeDtypeStruct((nb, 1, tm), jnp.int32),
                   jax.ShapeDtypeStruct((nb, 1, LANES), jnp.int32)],
        scratch_shapes=[pltpu.VMEM((LANES, tm), F32),
                        pltpu.VMEM((tm, sgu_dim), F32), pltpu.VMEM((tm, sgu_dim), F32),
                        pltpu.VMEM((tm, sgu_dim), BF16)],
        compiler_params=_params(1),
        name="sgu_merge",
    )(x2, o, *consts)


def _moe_kernel(nsub_ref, h_ref, rinfo_ref, dest_ref, wg_ref, wu_ref, wd_ref, g_ref, b_ref, out_ref, acc_ref,
                *, alpha, tm):
    blk = pl.program_id(0)
    h = h_ref[...]
    hb = h.astype(BF16)
    rinfo = rinfo_ref[...]
    lane = lax.broadcasted_iota(jnp.int32, rinfo.shape, 1)
    cw_parts = jnp.where(lane < 2 * EXPERTS_PER_GROUP, rinfo, 0.0).astype(BF16)
    dest_col = rinfo[:, 2 * EXPERTS_PER_GROUP:2 * EXPERTS_PER_GROUP + 1].astype(jnp.int32)
    dest_row = dest_ref[0]
    sub_row = lax.broadcasted_iota(jnp.int32, (SUB_ROWS, tm), 0)
    sub_lane = lax.broadcasted_iota(jnp.int32, (tm, SUB_ROWS), 1)

    def sub_tiles(groups, first_rows):
        n = len(groups)
        gather = [jnp.where(sub_row + r == dest_row, 1.0, 0.0).astype(BF16) for r in first_rows]
        scatter = [jnp.where(sub_lane + r == dest_col, 1.0, 0.0).astype(BF16) for r in first_rows]
        xs = [jnp.dot(gather[i], hb, preferred_element_type=F32).astype(BF16) for i in range(n)]
        cws = [jnp.dot(gather[i], cw_parts, preferred_element_type=F32) for i in range(n)]
        y = [None] * n
        for j in range(EXPERTS_PER_GROUP):
            hg = [jnp.dot(xs[i], wg_ref[groups[i] * EXPERTS_PER_GROUP + j], preferred_element_type=F32)
                  for i in range(n)]
            hu = [jnp.dot(xs[i], wu_ref[groups[i] * EXPERTS_PER_GROUP + j], preferred_element_type=F32)
                  for i in range(n)]
            for i in range(n):
                c_e = cws[i][:, j:j + 1] + cws[i][:, EXPERTS_PER_GROUP + j:EXPERTS_PER_GROUP + j + 1]
                hm = (_silu(hg[i]) * hu[i] * c_e).astype(BF16)
                part = jnp.dot(hm, wd_ref[groups[i] * EXPERTS_PER_GROUP + j], preferred_element_type=F32)
                y[i] = part if y[i] is None else y[i] + part
        out = [jnp.dot(scatter[i], y[i].astype(BF16), preferred_element_type=F32) for i in range(n)]
        return functools.reduce(jnp.add, out)

    n_sub = [nsub_ref[blk * LANES + g] for g in range(N_GROUPS)]
    first = [nsub_ref[blk * LANES + N_GROUPS + g] for g in range(N_GROUPS)]
    past_end = (tm // SUB_ROWS + N_GROUPS) * SUB_ROWS
    acc_ref[...] = sub_tiles(list(range(N_GROUPS)),
                             [jnp.where(n_sub[g] > 0, first[g] * SUB_ROWS, past_end) for g in range(N_GROUPS)])

    for g in range(N_GROUPS):
        def later_sub_tile(s, carry, g=g):
            acc_ref[...] += sub_tiles([g], [(first[g] + s) * SUB_ROWS])
            return carry

        lax.fori_loop(1, n_sub[g], later_sub_tile, 0)

    out_ref[...] = _layer_norm(alpha * h + acc_ref[...], g_ref[...], b_ref[...])


def _moe(h1, rinfo, dest, nsub, wg, wu, wd, ln_g, ln_b, *, tm, alpha):
    t, d = h1.shape
    nd = lambda shape: pl.BlockSpec(shape, lambda i, ns: (0,) * len(shape), pipeline_mode=pl.Buffered(1))
    tok = lambda i, ns: (i, 0)
    grid_spec = pltpu.PrefetchScalarGridSpec(
        num_scalar_prefetch=1,
        grid=(t // tm,),
        in_specs=[pl.BlockSpec((tm, d), tok), pl.BlockSpec((tm, LANES), tok),
                  pl.BlockSpec((1, 1, tm), lambda i, ns: (i, 0, 0)),
                  nd(wg.shape), nd(wu.shape), nd(wd.shape), nd(ln_g.shape), nd(ln_b.shape)],
        out_specs=pl.BlockSpec((tm, d), tok),
        scratch_shapes=[pltpu.VMEM((tm, d), F32)])
    return pl.pallas_call(
        functools.partial(_moe_kernel, alpha=alpha, tm=tm),
        grid_spec=grid_spec,
        out_shape=jax.ShapeDtypeStruct((t, d), F32),
        compiler_params=_params(1),
        name="moe",
    )(nsub.reshape(-1), h1, rinfo, dest, wg, wu, wd, ln_g, ln_b)


def _pad_lanes(a, width=LANES):
    return jnp.pad(a, [(0, 0)] * (a.ndim - 1) + [(0, width - a.shape[-1])])


def _layer(h, w_in, conv_w, a_log, dt_bias, dn_norm_g, w_dn_out, sgu_ln_g, sgu_ln_b,
           spatial_w, spatial_b, w_sgu_out, w_out, ln1_g, ln1_b,
           router_group_w, router_group_b, router_expert_w, router_expert_b,
           expert_w_gate, expert_w_up, expert_w_down, ln2_g, ln2_b, *, alpha):
    bsz, seq, d = h.shape
    dn_dim = w_dn_out.shape[0]
    sgu_dim = w_sgu_out.shape[0]
    x2 = h.reshape(bsz * seq, d)

    o_z = 3 * dn_dim
    o_a = o_z + dn_dim
    o_uv = o_a + 2 * DN_HEADS
    o_gate = o_uv + 2 * sgu_dim
    wqkv = w_in[:, :o_z].astype(BF16)
    wz = w_in[:, o_z:o_a].astype(BF16)
    wab = _pad_lanes(w_in[:, o_a:o_uv]).astype(BF16)
    wuv = w_in[:, o_uv:o_gate].astype(BF16)
    wgate = w_in[:, o_gate:].astype(BF16)
    gpar = jnp.stack([_pad_lanes(-jnp.exp(a_log.astype(F32))), _pad_lanes(dt_bias.astype(F32))])

    qkv, sz, gb = _proj_dn(x2, wqkv, wz, wab, conv_w, gpar, bsz=bsz, seq=seq, tm=256)
    o = _delta(qkv, sz, gb, dn_norm_g.reshape(1, -1), bsz=bsz, seq=seq, tc=256)
    wr = _pad_lanes(jnp.concatenate([router_group_w, router_expert_w], axis=1))
    wr = jnp.concatenate(_split(wr), axis=1)
    br = _pad_lanes(jnp.concatenate([router_group_b, router_expert_b]).reshape(1, -1))
    h1, rinfo, dest, nsub = _merge(x2, o, wuv, wgate, sgu_ln_g.reshape(1, -1), sgu_ln_b.reshape(1, -1),
                                   spatial_w, spatial_b.T, w_sgu_out.astype(BF16),
                                   w_dn_out.astype(BF16), w_out.astype(BF16),
                                   ln1_g.reshape(1, -1), ln1_b.reshape(1, -1), wr, br,
                                   tm=MOE_BLOCK, alpha=alpha)
    h2 = _moe(h1, rinfo, dest, nsub, expert_w_gate.astype(BF16), expert_w_up.astype(BF16),
              expert_w_down.astype(BF16), ln2_g.reshape(1, -1), ln2_b.reshape(1, -1),
              tm=MOE_BLOCK, alpha=alpha)
    return h2.reshape(bsz, seq, d)


def kernel(x, w_in, conv_w, a_log, dt_bias, dn_norm_g, w_dn_out, sgu_ln_g, sgu_ln_b, spatial_w, spatial_b, w_sgu_out, w_out, ln1_g, ln1_b, router_group_w, router_group_b, router_expert_w, router_expert_b, expert_w_gate, expert_w_up, expert_w_down, ln2_g, ln2_b):
    depth = w_in.shape[0]
    alpha = (2.0 * depth) ** 0.25
    h = x
    for l in range(depth):
        h = _layer(h, w_in[l], conv_w[l], a_log[l], dt_bias[l], dn_norm_g[l], w_dn_out[l],
                   sgu_ln_g[l], sgu_ln_b[l], spatial_w[l], spatial_b[l], w_sgu_out[l], w_out[l],
                   ln1_g[l], ln1_b[l], router_group_w[l], router_group_b[l],
                   router_expert_w[l], router_expert_b[l],
                   expert_w_gate[l], expert_w_up[l], expert_w_down[l], ln2_g[l], ln2_b[l], alpha=alpha)
    return h
```

```python
import functools

import jax
import jax.numpy as jnp
from jax import lax
from jax.experimental import pallas as pl
from jax.experimental.pallas import tpu as pltpu

F32 = jnp.float32
BF16 = jnp.bfloat16

LANES = 128
DN_HEADS = 8
DN_HEAD_DIM = 128
CONV_WIDTH = 4
SGU_GROUPS = 8
SGU_CHUNK = 128
N_GROUPS = 4
EXPERTS_PER_GROUP = 4
LN_EPS = 1e-5
NORM_EPS = 1e-6
DELTA_CHUNK = 128
CONV_PAD = 8
PROJ_TILE = 256
DELTA_TILE = 256
MOE_BLOCK = 512
SUB_ROWS = 160
VMEM_LIMIT = 56 * 1024 * 1024

NT_DIMS = (((1,), (1,)), ((), ()))
TN_DIMS = (((0,), (0,)), ((), ()))


def _dot_f32(a, b):
    return jnp.dot(a, b, preferred_element_type=F32, precision=lax.Precision.HIGHEST)


def _split(x):
    hi = x.astype(BF16)
    return hi, (x - hi.astype(F32)).astype(BF16)


def _dot3_rows(lhs, b_hi, b_lo):
    m = lhs[0].shape[0]
    n = len(lhs)
    parts = [_split(a) for a in lhs]
    his = [hi for hi, _ in parts]
    los = [lo for _, lo in parts]
    r_hi = jnp.dot(jnp.concatenate(his + los, axis=0), b_hi, preferred_element_type=F32)
    r_lo = jnp.dot(jnp.concatenate(his, axis=0) if n > 1 else his[0], b_lo, preferred_element_type=F32)
    return [r_hi[i * m:(i + 1) * m] + r_hi[(n + i) * m:(n + i + 1) * m] + r_lo[i * m:(i + 1) * m]
            for i in range(n)]


def _sigmoid(x):
    return 0.5 + 0.5 * jnp.tanh(0.5 * x)


def _silu(x):
    half = 0.5 * x
    return half + half * jnp.tanh(half)


def _const_spec(shape):
    nd = len(shape)
    return pl.BlockSpec(shape, lambda *_: (0,) * nd, pipeline_mode=pl.Buffered(1))


def _params(n_axes):
    return pltpu.CompilerParams(dimension_semantics=("arbitrary",) * n_axes,
                                vmem_limit_bytes=VMEM_LIMIT)


def _proj_dn_kernel(x_ref, wqkv32_ref, wz32_ref, wab32_ref, conv_ref, gpar_ref,
                    qkv_ref, sz_ref, gb_ref, ext_ref, wqkv_ref, wz_ref, wab_ref, *, tm, dn_dim):
    i = pl.program_id(1)
    xb = x_ref[...].astype(BF16)
    n_qkv = 3 * dn_dim

    @pl.when(jnp.logical_and(pl.program_id(0) == 0, i == 0))
    def _():
        wqkv_ref[...] = wqkv32_ref[...].astype(BF16)
        wz_ref[...] = wz32_ref[...].astype(BF16)
        wab_ref[...] = wab32_ref[...].astype(BF16)

    @pl.when(i == 0)
    def _():
        ext_ref[0:CONV_PAD, :] = jnp.zeros((CONV_PAD, n_qkv), F32)

    for n in range(0, n_qkv, 256):
        ext_ref[CONV_PAD:CONV_PAD + tm, n:n + 256] = jnp.dot(
            xb, wqkv_ref[:, n:n + 256], preferred_element_type=F32)

    for n in range(0, n_qkv, LANES):
        w = conv_ref[:, n:n + LANES]
        y = w[3:4, :] * ext_ref[CONV_PAD:CONV_PAD + tm, n:n + LANES]
        for j in range(1, CONV_WIDTH):
            y = y + w[3 - j:4 - j, :] * ext_ref[CONV_PAD - j:CONV_PAD - j + tm, n:n + LANES]
        y = _silu(y)
        if n < 2 * dn_dim:
            y = y * lax.rsqrt(jnp.sum(y * y, axis=-1, keepdims=True) + NORM_EPS)
            if n < dn_dim:
                y = y * (DN_HEAD_DIM ** -0.5)
        qkv_ref[:, n:n + LANES] = y.astype(BF16)

    ext_ref[0:CONV_PAD, :] = ext_ref[tm:tm + CONV_PAD, :]

    for n in range(0, dn_dim, 256):
        z = jnp.dot(xb, wz_ref[:, n:n + 256], preferred_element_type=F32)
        sz_ref[:, n:n + 256] = _silu(z).astype(BF16)

    ab = jnp.dot(xb, wab_ref[...], preferred_element_type=F32)
    neg_exp_alog = gpar_ref[0:1, :]
    dt_bias = gpar_ref[1:2, :]
    sp_in = ab + dt_bias
    softplus = jnp.maximum(sp_in, 0.0) + jnp.log1p(jnp.exp(-jnp.abs(sp_in)))
    lane = lax.broadcasted_iota(jnp.int32, ab.shape, 1)
    gb_ref[...] = jnp.where(lane < DN_HEADS, neg_exp_alog * softplus, _sigmoid(ab))


def _proj_dn(x2, w_in, conv_w, gpar, *, bsz, seq, tm, dn_dim):
    d = x2.shape[1]
    nl = seq // tm
    tok = lambda b, i: (b * nl + i, 0)
    col_block = lambda width, first_col: pl.BlockSpec(
        (d, width), lambda b, i: (0, first_col // width), pipeline_mode=pl.Buffered(1))
    assert (3 * dn_dim) % dn_dim == 0 and (4 * dn_dim) % LANES == 0
    return pl.pallas_call(
        functools.partial(_proj_dn_kernel, tm=tm, dn_dim=dn_dim),
        grid=(bsz, nl),
        in_specs=[pl.BlockSpec((tm, d), tok),
                  col_block(3 * dn_dim, 0), col_block(dn_dim, 3 * dn_dim), col_block(LANES, 4 * dn_dim),
                  _const_spec(conv_w.shape), _const_spec(gpar.shape)],
        out_specs=[pl.BlockSpec((tm, 3 * dn_dim), tok),
                   pl.BlockSpec((tm, dn_dim), tok),
                   pl.BlockSpec((tm, LANES), tok)],
        out_shape=[jax.ShapeDtypeStruct((bsz * seq, 3 * dn_dim), BF16),
                   jax.ShapeDtypeStruct((bsz * seq, dn_dim), BF16),
                   jax.ShapeDtypeStruct((bsz * seq, LANES), F32)],
        scratch_shapes=[pltpu.VMEM((tm + CONV_PAD, 3 * dn_dim), F32),
                        pltpu.VMEM((d, 3 * dn_dim), BF16), pltpu.VMEM((d, dn_dim), BF16),
                        pltpu.VMEM((d, LANES), BF16)],
        compiler_params=_params(2),
        name="proj_dn",
    )(x2, w_in, w_in, w_in, conv_w, gpar)


def _delta_kernel(q_ref, k_ref, v_ref, gb_ref, sz_ref, ng_ref, o_ref, state_ref, *, tc):
    c = DELTA_CHUNK
    hd = DN_HEAD_DIM
    pw = 2 * hd
    assert c == hd

    @pl.when(pl.program_id(1) == 0)
    def _():
        state_ref[...] = jnp.zeros(state_ref.shape, F32)

    row = lax.broadcasted_iota(jnp.int32, (c, pw), 0)
    lane = lax.broadcasted_iota(jnp.int32, (c, pw), 1)
    first = lane < hd
    col = jnp.where(first, lane, lane - hd)
    causal = row >= col
    strict = row > col
    row_s = lax.broadcasted_iota(jnp.int32, (pw, pw), 0)
    lane_s = lax.broadcasted_iota(jnp.int32, (pw, pw), 1)
    same_head = (row_s < hd) == (lane_s < hd)
    trow = lax.broadcasted_iota(jnp.int32, (c, c), 0)
    tcol = lax.broadcasted_iota(jnp.int32, (c, c), 1)
    tri = (trow >= tcol).astype(F32)
    norm_g = jnp.concatenate([ng_ref[...], ng_ref[...]], axis=1)

    def blockdiag(x):
        z = jnp.zeros_like(x)
        return jnp.concatenate([jnp.where(first, x, z), jnp.where(first, z, x)], axis=0)

    def pair_cols(a, j0, j1):
        return jnp.concatenate([jnp.broadcast_to(a[:, j0:j0 + 1], (c, hd)),
                                jnp.broadcast_to(a[:, j1:j1 + 1], (c, hd))], axis=1)

    def pair_rows(a, j0, j1, r):
        return jnp.concatenate([jnp.broadcast_to(a[r:r + 1, j0:j0 + 1], (1, hd)),
                                jnp.broadcast_to(a[r:r + 1, j1:j1 + 1], (1, hd))], axis=1)

    n_pairs = DN_HEADS // 2
    n_chunks = tc // c
    units = [(ci, p) for ci in range(n_chunks) for p in range(n_pairs)]

    gcums = []
    for ci in range(n_chunks):
        gb = gb_ref[ci * c:(ci + 1) * c, :]
        gcum = _dot_f32(tri, gb)
        gcums.append((gb, gcum, gcum.T))
    pre = {}
    for ci, p in units:
        gb, gcum, gcum_t = gcums[ci]
        h0, h1 = 2 * p, 2 * p + 1
        rows = slice(ci * c, (ci + 1) * c)
        cols = slice(p * pw, (p + 1) * pw)
        q = q_ref[rows, cols].astype(F32)
        k_b = k_ref[rows, cols]
        k = k_b.astype(F32)
        v = v_ref[rows, cols].astype(F32)
        g_col = pair_cols(gcum, h0, h1)
        g_row = jnp.concatenate([gcum_t[h0:h0 + 1, :], gcum_t[h1:h1 + 1, :]], axis=1)
        g_last = pair_rows(gcum, h0, h1, c - 1)
        beta = pair_cols(gb, DN_HEADS + h0, DN_HEADS + h1)
        decay = jnp.where(causal, jnp.exp(jnp.where(causal, g_col - g_row, 0.0)), 0.0)
        exp_g = jnp.exp(g_col)
        k_beta = k * beta
        aq = lax.dot_general(jnp.concatenate([k_beta, q], axis=0).astype(BF16), blockdiag(k_b),
                             NT_DIMS, preferred_element_type=F32)
        pre[ci, p] = dict(
            neg_a=jnp.where(strict, -(aq[:c] * decay), 0.0),
            qk=(aq[c:] * decay).astype(BF16),
            vb_bd=blockdiag((v * beta).astype(BF16)),
            kbe_bd=blockdiag((k_beta * exp_g).astype(BF16)),
            q_exp=(q * exp_g).astype(BF16),
            k_dec=(k * jnp.exp(g_last - g_col)).astype(BF16),
            exp_last=jnp.exp(g_last))

    half = c // 2
    row_h = lax.broadcasted_iota(jnp.int32, (half, pw), 0)
    lane_h = lax.broadcasted_iota(jnp.int32, (half, pw), 1)
    blk_id = lax.shift_right_logical(lane_h, half.bit_length() - 1)
    blk = [blk_id == r for r in range(pw // half)]
    left_h = (blk_id & 1) == 0
    eye_h = (row_h == (lane_h & (half - 1))).astype(F32)

    def quad_blockdiag(x):
        z = jnp.zeros_like(x)
        return jnp.concatenate([jnp.where(b, x, z) for b in blk], axis=0)

    def bd_parts(x):
        hi, lo = _split(x)
        return quad_blockdiag(hi), quad_blockdiag(lo)

    nil = {u: jnp.where(left_h, pre[u]["neg_a"][:half], pre[u]["neg_a"][half:]) for u in units}
    t_diag = {u: eye_h + nil[u] for u in units}
    for u in units:
        nil[u], = _dot3_rows([nil[u]], *bd_parts(nil[u]))
    levels = half.bit_length() - 2
    for lvl in range(levels):
        rhs = {u: bd_parts(nil[u]) for u in units}
        for u in units:
            if lvl + 1 < levels:
                t_inc, nil[u] = _dot3_rows([t_diag[u], nil[u]], *rhs[u])
            else:
                t_inc, = _dot3_rows([t_diag[u]], *rhs[u])
            t_diag[u] = t_diag[u] + t_inc
    t_inv = {}
    x21 = {}
    for u in units:
        n21 = jnp.where(left_h, pre[u]["neg_a"][half:], 0.0)
        x21[u], = _dot3_rows([n21], *bd_parts(t_diag[u]))
    for u in units:
        x_hi, x_lo = _split(x21[u])
        z = jnp.zeros_like(x_hi)
        rhs_hi = jnp.concatenate([z, jnp.where(blk[0], x_hi, z), z, jnp.where(blk[2], x_hi, z)], axis=0)
        rhs_lo = jnp.concatenate([z, jnp.where(blk[0], x_lo, z), z, jnp.where(blk[2], x_lo, z)], axis=0)
        t21, = _dot3_rows([t_diag[u]], rhs_hi, rhs_lo)
        t_inv[u] = jnp.concatenate([jnp.where(left_h, t_diag[u], 0.0),
                                    jnp.where(left_h, t21, t_diag[u])], axis=0)
    u_mat, w_mat = {}, {}
    for u in units:
        t_b = t_inv[u].astype(BF16)
        u_mat[u] = jnp.dot(t_b, pre[u]["vb_bd"], preferred_element_type=F32)
        w_mat[u] = jnp.dot(t_b, pre[u]["kbe_bd"], preferred_element_type=F32).astype(BF16)

    states = [state_ref[p] for p in range(n_pairs)]
    for ci in range(n_chunks):
        rows = slice(ci * c, (ci + 1) * c)
        for p in range(n_pairs):
            u = (ci, p)
            cols = slice(p * pw, (p + 1) * pw)
            state = states[p]
            ws_qs = jnp.dot(jnp.concatenate([w_mat[u], pre[u]["q_exp"]], axis=0), state.astype(BF16),
                            preferred_element_type=F32)
            v_new_b = (u_mat[u] - ws_qs[:c]).astype(BF16)
            out = ws_qs[c:] + jnp.dot(pre[u]["qk"], blockdiag(v_new_b), preferred_element_type=F32)
            kv = lax.dot_general(pre[u]["k_dec"], v_new_b, TN_DIMS, preferred_element_type=F32)
            states[p] = jnp.where(same_head, state * pre[u]["exp_last"] + kv, 0.0)
            sq = out * out
            ms = jnp.concatenate(
                [jnp.broadcast_to(jnp.mean(sq[:, :hd], axis=-1, keepdims=True), (c, hd)),
                 jnp.broadcast_to(jnp.mean(sq[:, hd:], axis=-1, keepdims=True), (c, hd))], axis=1)
            out = out * lax.rsqrt(ms + NORM_EPS) * norm_g * sz_ref[rows, cols].astype(F32)
            o_ref[rows, cols] = out.astype(BF16)
    for p in range(n_pairs):
        state_ref[p] = states[p]


def _delta(qkv, sz, gb, norm_g, *, bsz, seq, tc):
    dn_dim = sz.shape[1]
    nl = seq // tc
    tok = lambda b, i: (b * nl + i, 0)
    return pl.pallas_call(
        functools.partial(_delta_kernel, tc=tc),
        grid=(bsz, nl),
        in_specs=[pl.BlockSpec((tc, dn_dim), lambda b, i: (b * nl + i, 0)),
                  pl.BlockSpec((tc, dn_dim), lambda b, i: (b * nl + i, 1)),
                  pl.BlockSpec((tc, dn_dim), lambda b, i: (b * nl + i, 2)),
                  pl.BlockSpec((tc, LANES), tok),
                  pl.BlockSpec((tc, dn_dim), tok),
                  _const_spec(norm_g.shape)],
        out_specs=pl.BlockSpec((tc, dn_dim), tok),
        out_shape=jax.ShapeDtypeStruct((bsz * seq, dn_dim), BF16),
        scratch_shapes=[pltpu.VMEM((DN_HEADS // 2, 2 * DN_HEAD_DIM, 2 * DN_HEAD_DIM), F32)],
        compiler_params=_params(2),
        name="delta",
    )(qkv, qkv, qkv, gb, sz, norm_g)


def _sgu_body(x_ref, wuv_ref, wgate_ref, lng_ref, lnb_ref, sw_ref, sb_ref, wso_ref, u_ref, v_ref, gated_ref,
              *, tm, sgu_dim, d_model):
    xb = x_ref[...].astype(BF16)
    inv_sqrt2 = 2.0 ** -0.5
    for n in range(0, 2 * sgu_dim, 256):
        uv = jnp.dot(xb, wuv_ref[:, n:n + 256], preferred_element_type=F32)
        act = 0.5 * uv * (1.0 + lax.erf(uv * inv_sqrt2))
        if n < sgu_dim:
            u_ref[:, n:n + 256] = act
        else:
            v_ref[:, n - sgu_dim:n - sgu_dim + 256] = act

    v = v_ref[...]
    vc = v - jnp.mean(v, axis=-1, keepdims=True)
    var = jnp.mean(vc * vc, axis=-1, keepdims=True)
    v_ref[...] = vc * lax.rsqrt(var + LN_EPS) * lng_ref[...] + lnb_ref[...]

    row = lax.broadcasted_iota(jnp.int32, (SGU_CHUNK, SGU_CHUNK), 0)
    col = lax.broadcasted_iota(jnp.int32, (SGU_CHUNK, SGU_CHUNK), 1)
    causal = row >= col
    for g in range(SGU_GROUPS):
        cols = slice(g * LANES, (g + 1) * LANES)
        w_causal = jnp.where(causal, sw_ref[g], 0.0).astype(BF16)
        bias = sb_ref[:, g:g + 1]
        for ci in range(tm // SGU_CHUNK):
            rows = slice(ci * SGU_CHUNK, (ci + 1) * SGU_CHUNK)
            mixed = jnp.dot(w_causal, v_ref[rows, cols].astype(BF16), preferred_element_type=F32) + bias
            gated_ref[rows, cols] = (u_ref[rows, cols] * mixed).astype(BF16)

    gated = gated_ref[...]
    ys, sgd = [], []
    for n in range(0, d_model, 256):
        y_sgu = jnp.dot(gated, wso_ref[:, n:n + 256], preferred_element_type=F32)
        gate_sgu = jnp.dot(xb, wgate_ref[:, d_model + n:d_model + n + 256], preferred_element_type=F32)
        ys.append(_sigmoid(gate_sgu) * y_sgu)
        gate_dn = jnp.dot(xb, wgate_ref[:, n:n + 256], preferred_element_type=F32)
        sgd.append(_sigmoid(gate_dn))
    return jnp.concatenate(ys, axis=1), jnp.concatenate(sgd, axis=1)


def _layer_norm(r, g, b):
    rc = r - jnp.mean(r, axis=-1, keepdims=True)
    var = jnp.mean(rc * rc, axis=-1, keepdims=True)
    return rc * lax.rsqrt(var + LN_EPS) * g + b


def _merge_kernel(x_ref, o_ref, wuv_ref, wgate_ref, lng_ref, lnb_ref, sw_ref, sb_ref, wso_ref,
                  wdn_ref, wout_ref, g_ref, b_ref, wr_ref, br_ref,
                  h_ref, rinfo_ref, dest_ref, nsub_ref, info_ref, u_ref, v_ref, gated_ref,
                  *, alpha, tm, sgu_dim, d_model):
    ys, sgd = _sgu_body(x_ref, wuv_ref, wgate_ref, lng_ref, lnb_ref, sw_ref, sb_ref, wso_ref,
                        u_ref, v_ref, gated_ref, tm=tm, sgu_dim=sgu_dim, d_model=d_model)
    y_dn = jnp.dot(o_ref[...], wdn_ref[...], preferred_element_type=F32)
    y = sgd * y_dn + ys
    mix = jnp.dot(y.astype(BF16), wout_ref[...], preferred_element_type=F32)
    h = _layer_norm(alpha * x_ref[...] + mix, g_ref[...], b_ref[...])
    h_ref[...] = h

    h_hi, h_lo = _split(h)
    parts = jnp.dot(jnp.concatenate([h_hi, h_lo], axis=0), wr_ref[...], preferred_element_type=F32)
    logits = (parts[:tm, :LANES] + parts[tm:, :LANES]) + (parts[:tm, LANES:] + parts[tm:, LANES:]) + br_ref[...]
    lt = logits.T
    row_of = lambda i: lt[i:i + 1, :]
    g_logit = [row_of(g) for g in range(N_GROUPS)]
    g_max = functools.reduce(jnp.maximum, g_logit)
    g_idx = jnp.full(g_max.shape, N_GROUPS - 1, jnp.int32)
    for g in range(N_GROUPS - 2, -1, -1):
        g_idx = jnp.where(g_logit[g] == g_max, g, g_idx)
    group_p = 1.0 / functools.reduce(jnp.add, [jnp.exp(gl - g_max) for gl in g_logit])
    e_logit = []
    for j in range(EXPERTS_PER_GROUP):
        sel = row_of(N_GROUPS + (N_GROUPS - 1) * EXPERTS_PER_GROUP + j)
        for g in range(N_GROUPS - 2, -1, -1):
            sel = jnp.where(g_idx == g, row_of(N_GROUPS + g * EXPERTS_PER_GROUP + j), sel)
        e_logit.append(sel)
    e1 = functools.reduce(jnp.maximum, e_logit)
    j1 = jnp.full(e1.shape, EXPERTS_PER_GROUP - 1, jnp.int32)
    for j in range(EXPERTS_PER_GROUP - 2, -1, -1):
        j1 = jnp.where(e_logit[j] == e1, j, j1)
    rest = [jnp.where(j1 == j, -jnp.inf, e_logit[j]) for j in range(EXPERTS_PER_GROUP)]
    e2 = functools.reduce(jnp.maximum, rest)
    j2 = jnp.full(e2.shape, EXPERTS_PER_GROUP - 1, jnp.int32)
    for j in range(EXPERTS_PER_GROUP - 2, -1, -1):
        j2 = jnp.where(rest[j] == e2, j, j2)
    r = jnp.exp(e2 - e1)
    p1 = group_p / (1.0 + r)
    p2 = group_p * r / (1.0 + r)
    p1_hi = p1.astype(BF16).astype(F32)
    p2_hi = p2.astype(BF16).astype(F32)

    onehot = [jnp.where(g_idx == g, 1.0, 0.0) for g in range(N_GROUPS)]
    trow = lax.broadcasted_iota(jnp.int32, (tm, tm), 0)
    tcol = lax.broadcasted_iota(jnp.int32, (tm, tm), 1)
    earlier = jnp.where(trow < tcol, 1.0, 0.0).astype(BF16)
    onehot_rows = jnp.concatenate(onehot + [jnp.zeros((8 - N_GROUPS, tm), F32)], axis=0).astype(BF16)
    rank = jnp.dot(onehot_rows, earlier, preferred_element_type=F32)
    lane1 = lax.broadcasted_iota(jnp.int32, (1, LANES), 1)
    nsub_row = jnp.zeros((1, LANES), F32)
    dest_row = jnp.zeros((1, tm), F32)
    start = jnp.zeros((1, 1), F32)
    for g in range(N_GROUPS):
        count = jnp.sum(onehot[g], axis=-1, keepdims=True)
        n_sub = jnp.floor((count + (SUB_ROWS - 1)) * (1.0 / SUB_ROWS))
        dest_row = dest_row + onehot[g] * (start * SUB_ROWS + rank[g:g + 1, :])
        nsub_row = jnp.where(lane1 == g, n_sub, jnp.where(lane1 == N_GROUPS + g, start, nsub_row))
        start = start + n_sub
    dest_ref[0] = dest_row.astype(jnp.int32)
    nsub_ref[0] = nsub_row.astype(jnp.int32)
    info_ref[...] = jnp.zeros(info_ref.shape, F32)
    for j in range(EXPERTS_PER_GROUP):
        info_ref[j:j + 1, :] = jnp.where(j1 == j, p1_hi, jnp.where(j2 == j, p2_hi, 0.0))
        info_ref[EXPERTS_PER_GROUP + j:EXPERTS_PER_GROUP + j + 1, :] = jnp.where(
            j1 == j, p1 - p1_hi, jnp.where(j2 == j, p2 - p2_hi, 0.0))
    info_ref[2 * EXPERTS_PER_GROUP:2 * EXPERTS_PER_GROUP + 1, :] = dest_row
    rinfo_ref[...] = info_ref[...].T


def _merge(x2, o, wuv, wgate, sgu_ln_g, sgu_ln_b, spatial_w, spatial_b_t, wso, wdn, wout, ln_g, ln_b, wr, br,
           *, tm, alpha):
    t, d = x2.shape
    sgu_dim = wso.shape[0]
    nb = t // tm
    tok = lambda i: (i, 0)
    consts = (wuv, wgate, sgu_ln_g, sgu_ln_b, spatial_w, spatial_b_t, wso, wdn, wout, ln_g, ln_b, wr, br)
    return pl.pallas_call(
        functools.partial(_merge_kernel, alpha=alpha, tm=tm, sgu_dim=sgu_dim, d_model=d),
        grid=(nb,),
        in_specs=[pl.BlockSpec((tm, d), tok), pl.BlockSpec((tm, o.shape[1]), tok)]
        + [_const_spec(c.shape) for c in consts],
        out_specs=[pl.BlockSpec((tm, d), tok), pl.BlockSpec((tm, LANES), tok),
                   pl.BlockSpec((1, 1, tm), lambda i: (i, 0, 0)),
                   pl.BlockSpec((1, 1, LANES), lambda i: (i, 0, 0))],
        out_shape=[jax.ShapeDtypeStruct((t, d), F32), jax.ShapeDtypeStruct((t, LANES), F32),
                   jax.ShapeDtypeStruct((nb, 1, tm), jnp.int32),
                   jax.ShapeDtypeStruct((nb, 1, LANES), jnp.int32)],
        scratch_shapes=[pltpu.VMEM((LANES, tm), F32),
                        pltpu.VMEM((tm, sgu_dim), F32), pltpu.VMEM((tm, sgu_dim), F32),
                        pltpu.VMEM((tm, sgu_dim), BF16)],
        compiler_params=_params(1),
        name="sgu_merge",
    )(x2, o, *consts)


def _moe_kernel(nsub_ref, h_ref, rinfo_ref, dest_ref, wg_ref, wu_ref, wd_ref, g_ref, b_ref, out_ref, acc_ref,
                *, alpha, tm):
    blk = pl.program_id(0)
    h = h_ref[...]
    hb = h.astype(BF16)
    rinfo = rinfo_ref[...]
    lane = lax.broadcasted_iota(jnp.int32, rinfo.shape, 1)
    cw_parts = jnp.where(lane < 2 * EXPERTS_PER_GROUP, rinfo, 0.0).astype(BF16)
    dest_col = rinfo[:, 2 * EXPERTS_PER_GROUP:2 * EXPERTS_PER_GROUP + 1].astype(jnp.int32)
    dest_row = dest_ref[0]
    sub_row = lax.broadcasted_iota(jnp.int32, (SUB_ROWS, tm), 0)
    sub_lane = lax.broadcasted_iota(jnp.int32, (tm, SUB_ROWS), 1)

    def sub_tiles(groups, first_rows):
        n = len(groups)
        gather = [jnp.where(sub_row + r == dest_row, 1.0, 0.0).astype(BF16) for r in first_rows]
        scatter = [jnp.where(sub_lane + r == dest_col, 1.0, 0.0).astype(BF16) for r in first_rows]
        xs = [jnp.dot(gather[i], hb, preferred_element_type=F32).astype(BF16) for i in range(n)]
        cws = [jnp.dot(gather[i], cw_parts, preferred_element_type=F32) for i in range(n)]
        y = [None] * n
        for j in range(EXPERTS_PER_GROUP):
            hg = [jnp.dot(xs[i], wg_ref[groups[i] * EXPERTS_PER_GROUP + j], preferred_element_type=F32)
                  for i in range(n)]
            hu = [jnp.dot(xs[i], wu_ref[groups[i] * EXPERTS_PER_GROUP + j], preferred_element_type=F32)
                  for i in range(n)]
            for i in range(n):
                c_e = cws[i][:, j:j + 1] + cws[i][:, EXPERTS_PER_GROUP + j:EXPERTS_PER_GROUP + j + 1]
                hm = (_silu(hg[i]) * hu[i] * c_e).astype(BF16)
                part = jnp.dot(hm, wd_ref[groups[i] * EXPERTS_PER_GROUP + j], preferred_element_type=F32)
                y[i] = part if y[i] is None else y[i] + part
        out = [jnp.dot(scatter[i], y[i].astype(BF16), preferred_element_type=F32) for i in range(n)]
        return functools.reduce(jnp.add, out)

    n_sub = [nsub_ref[blk * LANES + g] for g in range(N_GROUPS)]
    first = [nsub_ref[blk * LANES + N_GROUPS + g] for g in range(N_GROUPS)]
    past_end = (tm // SUB_ROWS + N_GROUPS) * SUB_ROWS
    acc_ref[...] = sub_tiles(list(range(N_GROUPS)),
                             [jnp.where(n_sub[g] > 0, first[g] * SUB_ROWS, past_end) for g in range(N_GROUPS)])

    for g in range(N_GROUPS):
        def later_sub_tile(s, carry, g=g):
            acc_ref[...] += sub_tiles([g], [(first[g] + s) * SUB_ROWS])
            return carry

        lax.fori_loop(1, n_sub[g], later_sub_tile, 0)

    out_ref[...] = _layer_norm(alpha * h + acc_ref[...], g_ref[...], b_ref[...])


def _moe(h1, rinfo, dest, nsub, wg, wu, wd, ln_g, ln_b, *, tm, alpha):
    t, d = h1.shape
    nd = lambda shape: pl.BlockSpec(shape, lambda i, ns: (0,) * len(shape), pipeline_mode=pl.Buffered(1))
    tok = lambda i, ns: (i, 0)
    grid_spec = pltpu.PrefetchScalarGridSpec(
        num_scalar_prefetch=1,
        grid=(t // tm,),
        in_specs=[pl.BlockSpec((tm, d), tok), pl.BlockSpec((tm, LANES), tok),
                  pl.BlockSpec((1, 1, tm), lambda i, ns: (i, 0, 0)),
                  nd(wg.shape), nd(wu.shape), nd(wd.shape), nd(ln_g.shape), nd(ln_b.shape)],
        out_specs=pl.BlockSpec((tm, d), tok),
        scratch_shapes=[pltpu.VMEM((tm, d), F32)])
    return pl.pallas_call(
        functools.partial(_moe_kernel, alpha=alpha, tm=tm),
        grid_spec=grid_spec,
        out_shape=jax.ShapeDtypeStruct((t, d), F32),
        compiler_params=_params(1),
        name="moe",
    )(nsub.reshape(-1), h1, rinfo, dest, wg, wu, wd, ln_g, ln_b)


def _pad_lanes(a, width=LANES):
    return jnp.pad(a, [(0, 0)] * (a.ndim - 1) + [(0, width - a.shape[-1])])


def _layer(h, w_in, conv_w, a_log, dt_bias, dn_norm_g, w_dn_out, sgu_ln_g, sgu_ln_b,
           spatial_w, spatial_b, w_sgu_out, w_out, ln1_g, ln1_b,
           router_group_w, router_group_b, router_expert_w, router_expert_b,
           expert_w_gate, expert_w_up, expert_w_down, ln2_g, ln2_b, *, alpha):
    bsz, seq, d = h.shape
    dn_dim = w_dn_out.shape[0]
    sgu_dim = w_sgu_out.shape[0]
    x2 = h.reshape(bsz * seq, d)

    o_z = 3 * dn_dim
    o_a = o_z + dn_dim
    o_uv = o_a + 2 * DN_HEADS
    o_gate = o_uv + 2 * sgu_dim
    wuv = w_in[:, o_uv:o_gate].astype(BF16)
    wgate = w_in[:, o_gate:].astype(BF16)
    gpar = jnp.stack([_pad_lanes(-jnp.exp(a_log.astype(F32))), _pad_lanes(dt_bias.astype(F32))])

    qkv, sz, gb = _proj_dn(x2, w_in, conv_w, gpar, bsz=bsz, seq=seq, tm=PROJ_TILE, dn_dim=dn_dim)
    o = _delta(qkv, sz, gb, dn_norm_g.reshape(1, -1), bsz=bsz, seq=seq, tc=DELTA_TILE)
    wr = _pad_lanes(jnp.concatenate([router_group_w, router_expert_w], axis=1))
    wr = jnp.concatenate(_split(wr), axis=1)
    br = _pad_lanes(jnp.concatenate([router_group_b, router_expert_b]).reshape(1, -1))
    h1, rinfo, dest, nsub = _merge(x2, o, wuv, wgate, sgu_ln_g.reshape(1, -1), sgu_ln_b.reshape(1, -1),
                                   spatial_w, spatial_b.T, w_sgu_out.astype(BF16),
                                   w_dn_out.astype(BF16), w_out.astype(BF16),
                                   ln1_g.reshape(1, -1), ln1_b.reshape(1, -1), wr, br,
                                   tm=MOE_BLOCK, alpha=alpha)
    h2 = _moe(h1, rinfo, dest, nsub, expert_w_gate.astype(BF16), expert_w_up.astype(BF16),
              expert_w_down.astype(BF16), ln2_g.reshape(1, -1), ln2_b.reshape(1, -1),
              tm=MOE_BLOCK, alpha=alpha)
    return h2.reshape(bsz, seq, d)


def kernel(x, w_in, conv_w, a_log, dt_bias, dn_norm_g, w_dn_out, sgu_ln_g, sgu_ln_b, spatial_w, spatial_b, w_sgu_out, w_out, ln1_g, ln1_b, router_group_w, router_group_b, router_expert_w, router_expert_b, expert_w_gate, expert_w_up, expert_w_down, ln2_g, ln2_b):
    depth = w_in.shape[0]
    alpha = (2.0 * depth) ** 0.25
    h = x
    for l in range(depth):
        h = _layer(h, w_in[l], conv_w[l], a_log[l], dt_bias[l], dn_norm_g[l], w_dn_out[l],
                   sgu_ln_g[l], sgu_ln_b[l], spatial_w[l], spatial_b[l], w_sgu_out[l], w_out[l],
                   ln1_g[l], ln1_b[l], router_group_w[l], router_group_b[l],
                   router_expert_w[l], router_expert_b[l],
                   expert_w_gate[l], expert_w_up[l], expert_w_down[l], ln2_g[l], ln2_b[l], alpha=alpha)
    return h
```

```python
import functools

import jax
import jax.numpy as jnp
from jax import lax
from jax.experimental import pallas as pl
from jax.experimental.pallas import tpu as pltpu

F32 = jnp.float32
BF16 = jnp.bfloat16

LANES = 128
DN_HEADS = 8
DN_HEAD_DIM = 128
CONV_WIDTH = 4
SGU_GROUPS = 8
SGU_CHUNK = 128
N_GROUPS = 4
EXPERTS_PER_GROUP = 4
LN_EPS = 1e-5
NORM_EPS = 1e-6
DELTA_CHUNK = 128
INVERSE_BASE_BLOCK = 32
CONV_PAD = 8
PROJ_TILE = 256
DELTA_TILE = 512
MOE_BLOCK = 512
SUB_ROWS = 160
VMEM_LIMIT = 56 * 1024 * 1024

NT_DIMS = (((1,), (1,)), ((), ()))
TN_DIMS = (((0,), (0,)), ((), ()))


def _dot_f32(a, b):
    return jnp.dot(a, b, preferred_element_type=F32, precision=lax.Precision.HIGHEST)


def _split(x):
    hi = x.astype(BF16)
    return hi, (x - hi.astype(F32)).astype(BF16)


def _dot3_rows(lhs, b_hi, b_lo):
    m = lhs[0].shape[0]
    n = len(lhs)
    parts = [_split(a) for a in lhs]
    his = [hi for hi, _ in parts]
    los = [lo for _, lo in parts]
    r_hi = jnp.dot(jnp.concatenate(his + los, axis=0), b_hi, preferred_element_type=F32)
    r_lo = jnp.dot(jnp.concatenate(his, axis=0) if n > 1 else his[0], b_lo, preferred_element_type=F32)
    return [r_hi[i * m:(i + 1) * m] + r_hi[(n + i) * m:(n + i + 1) * m] + r_lo[i * m:(i + 1) * m]
            for i in range(n)]


def _sigmoid(x):
    return 0.5 + 0.5 * jnp.tanh(0.5 * x)


def _silu(x):
    half = 0.5 * x
    return half + half * jnp.tanh(half)


def _const_spec(shape):
    nd = len(shape)
    return pl.BlockSpec(shape, lambda *_: (0,) * nd, pipeline_mode=pl.Buffered(1))


def _params(n_axes):
    return pltpu.CompilerParams(dimension_semantics=("arbitrary",) * n_axes,
                                vmem_limit_bytes=VMEM_LIMIT)


def _proj_dn_kernel(x_ref, wqkv_ref, wz_ref, wab_ref, conv_ref, gpar_ref,
                    qkv_ref, sz_ref, gb_ref, ext_ref, *, tm, dn_dim):
    i = pl.program_id(1)
    xb = x_ref[...].astype(BF16)
    n_qkv = 3 * dn_dim

    @pl.when(i == 0)
    def _():
        ext_ref[0:CONV_PAD, :] = jnp.zeros((CONV_PAD, n_qkv), F32)

    for n in range(0, n_qkv, 256):
        ext_ref[CONV_PAD:CONV_PAD + tm, n:n + 256] = jnp.dot(
            xb, wqkv_ref[:, n:n + 256], preferred_element_type=F32)

    for n in range(0, n_qkv, LANES):
        w = conv_ref[:, n:n + LANES]
        y = w[3:4, :] * ext_ref[CONV_PAD:CONV_PAD + tm, n:n + LANES]
        for j in range(1, CONV_WIDTH):
            y = y + w[3 - j:4 - j, :] * ext_ref[CONV_PAD - j:CONV_PAD - j + tm, n:n + LANES]
        y = _silu(y)
        if n < 2 * dn_dim:
            y = y * lax.rsqrt(jnp.sum(y * y, axis=-1, keepdims=True) + NORM_EPS)
            if n < dn_dim:
                y = y * (DN_HEAD_DIM ** -0.5)
        qkv_ref[:, n:n + LANES] = y.astype(BF16)

    ext_ref[0:CONV_PAD, :] = ext_ref[tm:tm + CONV_PAD, :]

    for n in range(0, dn_dim, 256):
        z = jnp.dot(xb, wz_ref[:, n:n + 256], preferred_element_type=F32)
        sz_ref[:, n:n + 256] = _silu(z).astype(BF16)

    ab = jnp.dot(xb, wab_ref[...], preferred_element_type=F32)
    neg_exp_alog = gpar_ref[0:1, :]
    dt_bias = gpar_ref[1:2, :]
    sp_in = ab + dt_bias
    softplus = jnp.maximum(sp_in, 0.0) + jnp.log1p(jnp.exp(-jnp.abs(sp_in)))
    lane = lax.broadcasted_iota(jnp.int32, ab.shape, 1)
    gb_ref[...] = jnp.where(lane < DN_HEADS, neg_exp_alog * softplus, _sigmoid(ab))


def _proj_dn(x2, wqkv, wz, wab, conv_w, gpar, *, bsz, seq, tm):
    d = x2.shape[1]
    dn_dim = wz.shape[1]
    nl = seq // tm
    tok = lambda b, i: (b * nl + i, 0)
    return pl.pallas_call(
        functools.partial(_proj_dn_kernel, tm=tm, dn_dim=dn_dim),
        grid=(bsz, nl),
        in_specs=[pl.BlockSpec((tm, d), tok),
                  _const_spec(wqkv.shape), _const_spec(wz.shape), _const_spec(wab.shape),
                  _const_spec(conv_w.shape), _const_spec(gpar.shape)],
        out_specs=[pl.BlockSpec((tm, 3 * dn_dim), tok),
                   pl.BlockSpec((tm, dn_dim), tok),
                   pl.BlockSpec((tm, LANES), tok)],
        out_shape=[jax.ShapeDtypeStruct((bsz * seq, 3 * dn_dim), BF16),
                   jax.ShapeDtypeStruct((bsz * seq, dn_dim), BF16),
                   jax.ShapeDtypeStruct((bsz * seq, LANES), F32)],
        scratch_shapes=[pltpu.VMEM((tm + CONV_PAD, 3 * dn_dim), F32)],
        compiler_params=_params(2),
        name="proj_dn",
    )(x2, wqkv, wz, wab, conv_w, gpar)


def _delta_kernel(q_ref, k_ref, v_ref, gb_ref, sz_ref, ng_ref, o_ref, state_ref, *, tc):
    c = DELTA_CHUNK
    hd = DN_HEAD_DIM
    pw = 2 * hd
    assert c == hd

    @pl.when(pl.program_id(1) == 0)
    def _():
        state_ref[...] = jnp.zeros(state_ref.shape, F32)

    row = lax.broadcasted_iota(jnp.int32, (c, pw), 0)
    lane = lax.broadcasted_iota(jnp.int32, (c, pw), 1)
    first = lane < hd
    col = jnp.where(first, lane, lane - hd)
    causal = row >= col
    strict = row > col
    row_s = lax.broadcasted_iota(jnp.int32, (pw, pw), 0)
    lane_s = lax.broadcasted_iota(jnp.int32, (pw, pw), 1)
    same_head = (row_s < hd) == (lane_s < hd)
    trow = lax.broadcasted_iota(jnp.int32, (c, c), 0)
    tcol = lax.broadcasted_iota(jnp.int32, (c, c), 1)
    tri = (trow >= tcol).astype(F32)
    norm_g = jnp.concatenate([ng_ref[...], ng_ref[...]], axis=1)

    def blockdiag(x):
        z = jnp.zeros_like(x)
        return jnp.concatenate([jnp.where(first, x, z), jnp.where(first, z, x)], axis=0)

    def pair_cols(a, j0, j1):
        return jnp.concatenate([jnp.broadcast_to(a[:, j0:j0 + 1], (c, hd)),
                                jnp.broadcast_to(a[:, j1:j1 + 1], (c, hd))], axis=1)

    def pair_rows(a, j0, j1, r):
        return jnp.concatenate([jnp.broadcast_to(a[r:r + 1, j0:j0 + 1], (1, hd)),
                                jnp.broadcast_to(a[r:r + 1, j1:j1 + 1], (1, hd))], axis=1)

    n_pairs = DN_HEADS // 2
    n_chunks = tc // c
    units = [(ci, p) for ci in range(n_chunks) for p in range(n_pairs)]

    gcums = []
    for ci in range(n_chunks):
        gb = gb_ref[ci * c:(ci + 1) * c, :]
        gcum = _dot_f32(tri, gb)
        gcums.append((gb, gcum, gcum.T))
    pre = {}
    for ci, p in units:
        gb, gcum, gcum_t = gcums[ci]
        h0, h1 = 2 * p, 2 * p + 1
        rows = slice(ci * c, (ci + 1) * c)
        cols = slice(p * pw, (p + 1) * pw)
        q = q_ref[rows, cols].astype(F32)
        k_b = k_ref[rows, cols]
        k = k_b.astype(F32)
        v = v_ref[rows, cols].astype(F32)
        g_col = pair_cols(gcum, h0, h1)
        g_row = jnp.concatenate([gcum_t[h0:h0 + 1, :], gcum_t[h1:h1 + 1, :]], axis=1)
        g_last = pair_rows(gcum, h0, h1, c - 1)
        beta = pair_cols(gb, DN_HEADS + h0, DN_HEADS + h1)
        decay = jnp.where(causal, jnp.exp(jnp.where(causal, g_col - g_row, 0.0)), 0.0)
        exp_g = jnp.exp(g_col)
        k_beta = k * beta
        aq = lax.dot_general(jnp.concatenate([k_beta, q], axis=0).astype(BF16), blockdiag(k_b),
                             NT_DIMS, preferred_element_type=F32)
        pre[ci, p] = dict(
            neg_a=jnp.where(strict, -(aq[:c] * decay), 0.0),
            qk=(aq[c:] * decay).astype(BF16),
            vb_bd=blockdiag((v * beta).astype(BF16)),
            kbe_bd=blockdiag((k_beta * exp_g).astype(BF16)),
            q_exp=(q * exp_g).astype(BF16),
            k_dec=(k * jnp.exp(g_last - g_col)).astype(BF16),
            exp_last=jnp.exp(g_last))

    base = INVERSE_BASE_BLOCK

    def masks(b):
        row_b = lax.broadcasted_iota(jnp.int32, (b, pw), 0)
        lane_b = lax.broadcasted_iota(jnp.int32, (b, pw), 1)
        blk_id = lax.shift_right_logical(lane_b, b.bit_length() - 1)
        blocks = [blk_id == r for r in range(pw // b)]
        eye_b = (row_b == (lane_b & (b - 1))).astype(F32)
        return blocks, eye_b

    def lane_blockdiag(x, blocks):
        z = jnp.zeros_like(x)
        return jnp.concatenate([jnp.where(m, x, z) for m in blocks], axis=0)

    def bd_parts(x, blocks):
        hi, lo = _split(x)
        return lane_blockdiag(hi, blocks), lane_blockdiag(lo, blocks)

    def diag_blocks(a, b):
        lane_2b = lax.broadcasted_iota(jnp.int32, (b, pw), 1)
        upper = (lane_2b & (2 * b - 1)) < b
        return jnp.where(upper, a[:b], a[b:])

    packed = {c: {u: pre[u]["neg_a"] for u in units}}
    size = c
    while size > base:
        packed[size // 2] = {u: diag_blocks(packed[size][u], size // 2) for u in units}
        size //= 2

    blocks, eye_b = masks(base)
    nil = dict(packed[base])
    t_blk = {u: eye_b + nil[u] for u in units}
    for u in units:
        nil[u], = _dot3_rows([nil[u]], *bd_parts(nil[u], blocks))
    levels = base.bit_length() - 2
    for lvl in range(levels):
        rhs = {u: bd_parts(nil[u], blocks) for u in units}
        for u in units:
            if lvl + 1 < levels:
                t_inc, nil[u] = _dot3_rows([t_blk[u], nil[u]], *rhs[u])
            else:
                t_inc, = _dot3_rows([t_blk[u]], *rhs[u])
            t_blk[u] = t_blk[u] + t_inc

    size = base
    while size < c:
        blocks, _ = masks(size)
        lane_s = lax.broadcasted_iota(jnp.int32, (size, pw), 1)
        left = (lane_s & (2 * size - 1)) < size
        x21 = {}
        for u in units:
            n21 = jnp.where(left, packed[2 * size][u][size:], 0.0)
            x21[u], = _dot3_rows([n21], *bd_parts(t_blk[u], blocks))
        grown = {}
        for u in units:
            x_hi, x_lo = _split(x21[u])
            z = jnp.zeros_like(x_hi)
            rhs_hi = jnp.concatenate([jnp.where(blocks[r - 1], x_hi, z) if r % 2 else z
                                      for r in range(pw // size)], axis=0)
            rhs_lo = jnp.concatenate([jnp.where(blocks[r - 1], x_lo, z) if r % 2 else z
                                      for r in range(pw // size)], axis=0)
            t21, = _dot3_rows([t_blk[u]], rhs_hi, rhs_lo)
            grown[u] = jnp.concatenate([jnp.where(left, t_blk[u], 0.0),
                                        jnp.where(left, t21, t_blk[u])], axis=0)
        t_blk = grown
        size *= 2
    t_inv = t_blk
    u_mat, w_mat = {}, {}
    for u in units:
        t_b = t_inv[u].astype(BF16)
        u_mat[u] = jnp.dot(t_b, pre[u]["vb_bd"], preferred_element_type=F32)
        w_mat[u] = jnp.dot(t_b, pre[u]["kbe_bd"], preferred_element_type=F32).astype(BF16)

    states = [state_ref[p] for p in range(n_pairs)]
    for ci in range(n_chunks):
        rows = slice(ci * c, (ci + 1) * c)
        for p in range(n_pairs):
            u = (ci, p)
            cols = slice(p * pw, (p + 1) * pw)
            state = states[p]
            ws_qs = jnp.dot(jnp.concatenate([w_mat[u], pre[u]["q_exp"]], axis=0), state.astype(BF16),
                            preferred_element_type=F32)
            v_new_b = (u_mat[u] - ws_qs[:c]).astype(BF16)
            out = ws_qs[c:] + jnp.dot(pre[u]["qk"], blockdiag(v_new_b), preferred_element_type=F32)
            kv = lax.dot_general(pre[u]["k_dec"], v_new_b, TN_DIMS, preferred_element_type=F32)
            states[p] = jnp.where(same_head, state * pre[u]["exp_last"] + kv, 0.0)
            sq = out * out
            ms = jnp.concatenate(
                [jnp.broadcast_to(jnp.mean(sq[:, :hd], axis=-1, keepdims=True), (c, hd)),
                 jnp.broadcast_to(jnp.mean(sq[:, hd:], axis=-1, keepdims=True), (c, hd))], axis=1)
            out = out * lax.rsqrt(ms + NORM_EPS) * norm_g * sz_ref[rows, cols].astype(F32)
            o_ref[rows, cols] = out.astype(BF16)
    for p in range(n_pairs):
        state_ref[p] = states[p]


def _delta(qkv, sz, gb, norm_g, *, bsz, seq, tc):
    dn_dim = sz.shape[1]
    nl = seq // tc
    tok = lambda b, i: (b * nl + i, 0)
    return pl.pallas_call(
        functools.partial(_delta_kernel, tc=tc),
        grid=(bsz, nl),
        in_specs=[pl.BlockSpec((tc, dn_dim), lambda b, i: (b * nl + i, 0)),
                  pl.BlockSpec((tc, dn_dim), lambda b, i: (b * nl + i, 1)),
                  pl.BlockSpec((tc, dn_dim), lambda b, i: (b * nl + i, 2)),
                  pl.BlockSpec((tc, LANES), tok),
                  pl.BlockSpec((tc, dn_dim), tok),
                  _const_spec(norm_g.shape)],
        out_specs=pl.BlockSpec((tc, dn_dim), tok),
        out_shape=jax.ShapeDtypeStruct((bsz * seq, dn_dim), BF16),
        scratch_shapes=[pltpu.VMEM((DN_HEADS // 2, 2 * DN_HEAD_DIM, 2 * DN_HEAD_DIM), F32)],
        compiler_params=_params(2),
        name="delta",
    )(qkv, qkv, qkv, gb, sz, norm_g)


def _sgu_body(x_ref, wuv_ref, wgate_ref, lng_ref, lnb_ref, sw_ref, sb_ref, wso_ref, u_ref, v_ref, gated_ref,
              *, tm, sgu_dim, d_model):
    xb = x_ref[...].astype(BF16)
    inv_sqrt2 = 2.0 ** -0.5
    for n in range(0, 2 * sgu_dim, 256):
        uv = jnp.dot(xb, wuv_ref[:, n:n + 256], preferred_element_type=F32)
        act = 0.5 * uv * (1.0 + lax.erf(uv * inv_sqrt2))
        if n < sgu_dim:
            u_ref[:, n:n + 256] = act
        else:
            v_ref[:, n - sgu_dim:n - sgu_dim + 256] = act

    v = v_ref[...]
    vc = v - jnp.mean(v, axis=-1, keepdims=True)
    var = jnp.mean(vc * vc, axis=-1, keepdims=True)
    v_ref[...] = vc * lax.rsqrt(var + LN_EPS) * lng_ref[...] + lnb_ref[...]

    row = lax.broadcasted_iota(jnp.int32, (SGU_CHUNK, SGU_CHUNK), 0)
    col = lax.broadcasted_iota(jnp.int32, (SGU_CHUNK, SGU_CHUNK), 1)
    causal = row >= col
    for g in range(SGU_GROUPS):
        cols = slice(g * LANES, (g + 1) * LANES)
        w_causal = jnp.where(causal, sw_ref[g], 0.0).astype(BF16)
        bias = sb_ref[:, g:g + 1]
        for ci in range(tm // SGU_CHUNK):
            rows = slice(ci * SGU_CHUNK, (ci + 1) * SGU_CHUNK)
            mixed = jnp.dot(w_causal, v_ref[rows, cols].astype(BF16), preferred_element_type=F32) + bias
            gated_ref[rows, cols] = (u_ref[rows, cols] * mixed).astype(BF16)

    gated = gated_ref[...]
    ys, sgd = [], []
    for n in range(0, d_model, 256):
        y_sgu = jnp.dot(gated, wso_ref[:, n:n + 256], preferred_element_type=F32)
        gate_sgu = jnp.dot(xb, wgate_ref[:, d_model + n:d_model + n + 256], preferred_element_type=F32)
        ys.append(_sigmoid(gate_sgu) * y_sgu)
        gate_dn = jnp.dot(xb, wgate_ref[:, n:n + 256], preferred_element_type=F32)
        sgd.append(_sigmoid(gate_dn))
    return jnp.concatenate(ys, axis=1), jnp.concatenate(sgd, axis=1)


def _layer_norm(r, g, b):
    rc = r - jnp.mean(r, axis=-1, keepdims=True)
    var = jnp.mean(rc * rc, axis=-1, keepdims=True)
    return rc * lax.rsqrt(var + LN_EPS) * g + b


def _merge_kernel(x_ref, o_ref, wuv_ref, wgate_ref, lng_ref, lnb_ref, sw_ref, sb_ref, wso_ref,
                  wdn_ref, wout_ref, g_ref, b_ref, wr_ref, br_ref,
                  h_ref, rinfo_ref, dest_ref, nsub_ref, info_ref, u_ref, v_ref, gated_ref,
                  *, alpha, tm, sgu_dim, d_model):
    ys, sgd = _sgu_body(x_ref, wuv_ref, wgate_ref, lng_ref, lnb_ref, sw_ref, sb_ref, wso_ref,
                        u_ref, v_ref, gated_ref, tm=tm, sgu_dim=sgu_dim, d_model=d_model)
    y_dn = jnp.dot(o_ref[...], wdn_ref[...], preferred_element_type=F32)
    y = sgd * y_dn + ys
    mix = jnp.dot(y.astype(BF16), wout_ref[...], preferred_element_type=F32)
    h = _layer_norm(alpha * x_ref[...] + mix, g_ref[...], b_ref[...])
    h_ref[...] = h

    h_hi, h_lo = _split(h)
    parts = jnp.dot(jnp.concatenate([h_hi, h_lo], axis=0), wr_ref[...], preferred_element_type=F32)
    logits = (parts[:tm, :LANES] + parts[tm:, :LANES]) + (parts[:tm, LANES:] + parts[tm:, LANES:]) + br_ref[...]
    lt = logits.T
    row_of = lambda i: lt[i:i + 1, :]
    g_logit = [row_of(g) for g in range(N_GROUPS)]
    g_max = functools.reduce(jnp.maximum, g_logit)
    g_idx = jnp.full(g_max.shape, N_GROUPS - 1, jnp.int32)
    for g in range(N_GROUPS - 2, -1, -1):
        g_idx = jnp.where(g_logit[g] == g_max, g, g_idx)
    group_p = 1.0 / functools.reduce(jnp.add, [jnp.exp(gl - g_max) for gl in g_logit])
    e_logit = []
    for j in range(EXPERTS_PER_GROUP):
        sel = row_of(N_GROUPS + (N_GROUPS - 1) * EXPERTS_PER_GROUP + j)
        for g in range(N_GROUPS - 2, -1, -1):
            sel = jnp.where(g_idx == g, row_of(N_GROUPS + g * EXPERTS_PER_GROUP + j), sel)
        e_logit.append(sel)
    e1 = functools.reduce(jnp.maximum, e_logit)
    j1 = jnp.full(e1.shape, EXPERTS_PER_GROUP - 1, jnp.int32)
    for j in range(EXPERTS_PER_GROUP - 2, -1, -1):
        j1 = jnp.where(e_logit[j] == e1, j, j1)
    rest = [jnp.where(j1 == j, -jnp.inf, e_logit[j]) for j in range(EXPERTS_PER_GROUP)]
    e2 = functools.reduce(jnp.maximum, rest)
    j2 = jnp.full(e2.shape, EXPERTS_PER_GROUP - 1, jnp.int32)
    for j in range(EXPERTS_PER_GROUP - 2, -1, -1):
        j2 = jnp.where(rest[j] == e2, j, j2)
    r = jnp.exp(e2 - e1)
    p1 = group_p / (1.0 + r)
    p2 = group_p * r / (1.0 + r)
    p1_hi = p1.astype(BF16).astype(F32)
    p2_hi = p2.astype(BF16).astype(F32)

    onehot = [jnp.where(g_idx == g, 1.0, 0.0) for g in range(N_GROUPS)]
    trow = lax.broadcasted_iota(jnp.int32, (tm, tm), 0)
    tcol = lax.broadcasted_iota(jnp.int32, (tm, tm), 1)
    earlier = jnp.where(trow < tcol, 1.0, 0.0).astype(BF16)
    onehot_rows = jnp.concatenate(onehot + [jnp.zeros((8 - N_GROUPS, tm), F32)], axis=0).astype(BF16)
    rank = jnp.dot(onehot_rows, earlier, preferred_element_type=F32)
    lane1 = lax.broadcasted_iota(jnp.int32, (1, LANES), 1)
    nsub_row = jnp.zeros((1, LANES), F32)
    dest_row = jnp.zeros((1, tm), F32)
    start = jnp.zeros((1, 1), F32)
    for g in range(N_GROUPS):
        count = jnp.sum(onehot[g], axis=-1, keepdims=True)
        n_sub = jnp.floor((count + (SUB_ROWS - 1)) * (1.0 / SUB_ROWS))
        dest_row = dest_row + onehot[g] * (start * SUB_ROWS + rank[g:g + 1, :])
        nsub_row = jnp.where(lane1 == g, n_sub, jnp.where(lane1 == N_GROUPS + g, start, nsub_row))
        start = start + n_sub
    dest_ref[0] = dest_row.astype(jnp.int32)
    nsub_ref[0] = nsub_row.astype(jnp.int32)
    info_ref[...] = jnp.zeros(info_ref.shape, F32)
    for j in range(EXPERTS_PER_GROUP):
        info_ref[j:j + 1, :] = jnp.where(j1 == j, p1_hi, jnp.where(j2 == j, p2_hi, 0.0))
        info_ref[EXPERTS_PER_GROUP + j:EXPERTS_PER_GROUP + j + 1, :] = jnp.where(
            j1 == j, p1 - p1_hi, jnp.where(j2 == j, p2 - p2_hi, 0.0))
    info_ref[2 * EXPERTS_PER_GROUP:2 * EXPERTS_PER_GROUP + 1, :] = dest_row
    rinfo_ref[...] = info_ref[...].T


def _merge(x2, o, wuv, wgate, sgu_ln_g, sgu_ln_b, spatial_w, spatial_b_t, wso, wdn, wout, ln_g, ln_b, wr, br,
           *, tm, alpha):
    t, d = x2.shape
    sgu_dim = wso.shape[0]
    nb = t // tm
    tok = lambda i: (i, 0)
    consts = (wuv, wgate, sgu_ln_g, sgu_ln_b, spatial_w, spatial_b_t, wso, wdn, wout, ln_g, ln_b, wr, br)
    return pl.pallas_call(
        functools.partial(_merge_kernel, alpha=alpha, tm=tm, sgu_dim=sgu_dim, d_model=d),
        grid=(nb,),
        in_specs=[pl.BlockSpec((tm, d), tok), pl.BlockSpec((tm, o.shape[1]), tok)]
        + [_const_spec(c.shape) for c in consts],
        out_specs=[pl.BlockSpec((tm, d), tok), pl.BlockSpec((tm, LANES), tok),
                   pl.BlockSpec((1, 1, tm), lambda i: (i, 0, 0)),
                   pl.BlockSpec((1, 1, LANES), lambda i: (i, 0, 0))],
        out_shape=[jax.ShapeDtypeStruct((t, d), F32), jax.ShapeDtypeStruct((t, LANES), F32),
                   jax.ShapeDtypeStruct((nb, 1, tm), jnp.int32),
                   jax.ShapeDtypeStruct((nb, 1, LANES), jnp.int32)],
        scratch_shapes=[pltpu.VMEM((LANES, tm), F32),
                        pltpu.VMEM((tm, sgu_dim), F32), pltpu.VMEM((tm, sgu_dim), F32),
                        pltpu.VMEM((tm, sgu_dim), BF16)],
        compiler_params=_params(1),
        name="sgu_merge",
    )(x2, o, *consts)


def _moe_kernel(nsub_ref, h_ref, rinfo_ref, dest_ref, wg_ref, wu_ref, wd_ref, g_ref, b_ref, out_ref, acc_ref,
                *, alpha, tm):
    blk = pl.program_id(0)
    h = h_ref[...]
    hb = h.astype(BF16)
    rinfo = rinfo_ref[...]
    lane = lax.broadcasted_iota(jnp.int32, rinfo.shape, 1)
    cw_parts = jnp.where(lane < 2 * EXPERTS_PER_GROUP, rinfo, 0.0).astype(BF16)
    dest_col = rinfo[:, 2 * EXPERTS_PER_GROUP:2 * EXPERTS_PER_GROUP + 1].astype(jnp.int32)
    dest_row = dest_ref[0]
    sub_row = lax.broadcasted_iota(jnp.int32, (SUB_ROWS, tm), 0)
    sub_lane = lax.broadcasted_iota(jnp.int32, (tm, SUB_ROWS), 1)

    def sub_tiles(groups, first_rows):
        n = len(groups)
        gather = [jnp.where(sub_row + r == dest_row, 1.0, 0.0).astype(BF16) for r in first_rows]
        scatter = [jnp.where(sub_lane + r == dest_col, 1.0, 0.0).astype(BF16) for r in first_rows]
        xs = [jnp.dot(gather[i], hb, preferred_element_type=F32).astype(BF16) for i in range(n)]
        cws = [jnp.dot(gather[i], cw_parts, preferred_element_type=F32) for i in range(n)]
        y = [None] * n
        for j in range(EXPERTS_PER_GROUP):
            hg = [jnp.dot(xs[i], wg_ref[groups[i] * EXPERTS_PER_GROUP + j], preferred_element_type=F32)
                  for i in range(n)]
            hu = [jnp.dot(xs[i], wu_ref[groups[i] * EXPERTS_PER_GROUP + j], preferred_element_type=F32)
                  for i in range(n)]
            for i in range(n):
                c_e = cws[i][:, j:j + 1] + cws[i][:, EXPERTS_PER_GROUP + j:EXPERTS_PER_GROUP + j + 1]
                hm = (_silu(hg[i]) * hu[i] * c_e).astype(BF16)
                part = jnp.dot(hm, wd_ref[groups[i] * EXPERTS_PER_GROUP + j], preferred_element_type=F32)
                y[i] = part if y[i] is None else y[i] + part
        out = [jnp.dot(scatter[i], y[i].astype(BF16), preferred_element_type=F32) for i in range(n)]
        return functools.reduce(jnp.add, out)

    n_sub = [nsub_ref[blk * LANES + g] for g in range(N_GROUPS)]
    first = [nsub_ref[blk * LANES + N_GROUPS + g] for g in range(N_GROUPS)]
    past_end = (tm // SUB_ROWS + N_GROUPS) * SUB_ROWS
    acc_ref[...] = sub_tiles(list(range(N_GROUPS)),
                             [jnp.where(n_sub[g] > 0, first[g] * SUB_ROWS, past_end) for g in range(N_GROUPS)])

    for g in range(N_GROUPS):
        def later_sub_tile(s, carry, g=g):
            acc_ref[...] += sub_tiles([g], [(first[g] + s) * SUB_ROWS])
            return carry

        lax.fori_loop(1, n_sub[g], later_sub_tile, 0)

    out_ref[...] = _layer_norm(alpha * h + acc_ref[...], g_ref[...], b_ref[...])


def _moe(h1, rinfo, dest, nsub, wg, wu, wd, ln_g, ln_b, *, tm, alpha):
    t, d = h1.shape
    nd = lambda shape: pl.BlockSpec(shape, lambda i, ns: (0,) * len(shape), pipeline_mode=pl.Buffered(1))
    tok = lambda i, ns: (i, 0)
    grid_spec = pltpu.PrefetchScalarGridSpec(
        num_scalar_prefetch=1,
        grid=(t // tm,),
        in_specs=[pl.BlockSpec((tm, d), tok), pl.BlockSpec((tm, LANES), tok),
                  pl.BlockSpec((1, 1, tm), lambda i, ns: (i, 0, 0)),
                  nd(wg.shape), nd(wu.shape), nd(wd.shape), nd(ln_g.shape), nd(ln_b.shape)],
        out_specs=pl.BlockSpec((tm, d), tok),
        scratch_shapes=[pltpu.VMEM((tm, d), F32)])
    return pl.pallas_call(
        functools.partial(_moe_kernel, alpha=alpha, tm=tm),
        grid_spec=grid_spec,
        out_shape=jax.ShapeDtypeStruct((t, d), F32),
        compiler_params=_params(1),
        name="moe",
    )(nsub.reshape(-1), h1, rinfo, dest, wg, wu, wd, ln_g, ln_b)


def _pad_lanes(a, width=LANES):
    return jnp.pad(a, [(0, 0)] * (a.ndim - 1) + [(0, width - a.shape[-1])])


def _layer(h, w_in, conv_w, a_log, dt_bias, dn_norm_g, w_dn_out, sgu_ln_g, sgu_ln_b,
           spatial_w, spatial_b, w_sgu_out, w_out, ln1_g, ln1_b,
           router_group_w, router_group_b, router_expert_w, router_expert_b,
           expert_w_gate, expert_w_up, expert_w_down, ln2_g, ln2_b, *, alpha):
    bsz, seq, d = h.shape
    dn_dim = w_dn_out.shape[0]
    sgu_dim = w_sgu_out.shape[0]
    x2 = h.reshape(bsz * seq, d)

    o_z = 3 * dn_dim
    o_a = o_z + dn_dim
    o_uv = o_a + 2 * DN_HEADS
    o_gate = o_uv + 2 * sgu_dim
    wqkv = w_in[:, :o_z].astype(BF16)
    wz = w_in[:, o_z:o_a].astype(BF16)
    wab = _pad_lanes(w_in[:, o_a:o_uv]).astype(BF16)
    wuv = w_in[:, o_uv:o_gate].astype(BF16)
    wgate = w_in[:, o_gate:].astype(BF16)
    gpar = jnp.stack([_pad_lanes(-jnp.exp(a_log.astype(F32))), _pad_lanes(dt_bias.astype(F32))])

    qkv, sz, gb = _proj_dn(x2, wqkv, wz, wab, conv_w, gpar, bsz=bsz, seq=seq, tm=PROJ_TILE)
    o = _delta(qkv, sz, gb, dn_norm_g.reshape(1, -1), bsz=bsz, seq=seq, tc=DELTA_TILE)
    wr = _pad_lanes(jnp.concatenate([router_group_w, router_expert_w], axis=1))
    wr = jnp.concatenate(_split(wr), axis=1)
    br = _pad_lanes(jnp.concatenate([router_group_b, router_expert_b]).reshape(1, -1))
    h1, rinfo, dest, nsub = _merge(x2, o, wuv, wgate, sgu_ln_g.reshape(1, -1), sgu_ln_b.reshape(1, -1),
                                   spatial_w, spatial_b.T, w_sgu_out.astype(BF16),
                                   w_dn_out.astype(BF16), w_out.astype(BF16),
                                   ln1_g.reshape(1, -1), ln1_b.reshape(1, -1), wr, br,
                                   tm=MOE_BLOCK, alpha=alpha)
    h2 = _moe(h1, rinfo, dest, nsub, expert_w_gate.astype(BF16), expert_w_up.astype(BF16),
              expert_w_down.astype(BF16), ln2_g.reshape(1, -1), ln2_b.reshape(1, -1),
              tm=MOE_BLOCK, alpha=alpha)
    return h2.reshape(bsz, seq, d)


def kernel(x, w_in, conv_w, a_log, dt_bias, dn_norm_g, w_dn_out, sgu_ln_g, sgu_ln_b, spatial_w, spatial_b, w_sgu_out, w_out, ln1_g, ln1_b, router_group_w, router_group_b, router_expert_w, router_expert_b, expert_w_gate, expert_w_up, expert_w_down, ln2_g, ln2_b):
    depth = w_in.shape[0]
    alpha = (2.0 * depth) ** 0.25
    h = x
    for l in range(depth):
        h = _layer(h, w_in[l], conv_w[l], a_log[l], dt_bias[l], dn_norm_g[l], w_dn_out[l],
                   sgu_ln_g[l], sgu_ln_b[l], spatial_w[l], spatial_b[l], w_sgu_out[l], w_out[l],
                   ln1_g[l], ln1_b[l], router_group_w[l], router_group_b[l],
                   router_expert_w[l], router_expert_b[l],
                   expert_w_gate[l], expert_w_up[l], expert_w_down[l], ln2_g[l], ln2_b[l], alpha=alpha)
    return h
```

```python
import functools

import jax
import jax.numpy as jnp
from jax import lax
from jax.experimental import pallas as pl
from jax.experimental.pallas import tpu as pltpu

F32 = jnp.float32
BF16 = jnp.bfloat16

LANES = 128
DN_HEADS = 8
DN_HEAD_DIM = 128
CONV_WIDTH = 4
SGU_GROUPS = 8
SGU_CHUNK = 128
N_GROUPS = 4
EXPERTS_PER_GROUP = 4
LN_EPS = 1e-5
NORM_EPS = 1e-6
DELTA_CHUNK = 128
INVERSE_BASE_BLOCK = 32
CONV_PAD = 8
PROJ_TILE = 512
DELTA_TILE = 512
MOE_BLOCK = 512
SUB_ROWS = 160
VMEM_LIMIT = 56 * 1024 * 1024

NT_DIMS = (((1,), (1,)), ((), ()))
TN_DIMS = (((0,), (0,)), ((), ()))


def _dot_f32(a, b):
    return jnp.dot(a, b, preferred_element_type=F32, precision=lax.Precision.HIGHEST)


def _split(x):
    hi = x.astype(BF16)
    return hi, (x - hi.astype(F32)).astype(BF16)


def _dot3_rows(lhs, b_hi, b_lo):
    m = lhs[0].shape[0]
    n = len(lhs)
    parts = [_split(a) for a in lhs]
    his = [hi for hi, _ in parts]
    los = [lo for _, lo in parts]
    r_hi = jnp.dot(jnp.concatenate(his + los, axis=0), b_hi, preferred_element_type=F32)
    r_lo = jnp.dot(jnp.concatenate(his, axis=0) if n > 1 else his[0], b_lo, preferred_element_type=F32)
    return [r_hi[i * m:(i + 1) * m] + r_hi[(n + i) * m:(n + i + 1) * m] + r_lo[i * m:(i + 1) * m]
            for i in range(n)]


def _sigmoid(x):
    return 0.5 + 0.5 * jnp.tanh(0.5 * x)


def _silu(x):
    half = 0.5 * x
    return half + half * jnp.tanh(half)


def _const_spec(shape):
    nd = len(shape)
    return pl.BlockSpec(shape, lambda *_: (0,) * nd, pipeline_mode=pl.Buffered(1))


def _params(n_axes):
    return pltpu.CompilerParams(dimension_semantics=("arbitrary",) * n_axes,
                                vmem_limit_bytes=VMEM_LIMIT)


def _proj_dn_kernel(x_ref, wqkv_ref, wz_ref, wab_ref, conv_ref, gpar_ref,
                    qkv_ref, sz_ref, gb_ref, ext_ref, *, tm, dn_dim):
    i = pl.program_id(1)
    xb = x_ref[...].astype(BF16)
    n_qkv = 3 * dn_dim

    @pl.when(i == 0)
    def _():
        ext_ref[0:CONV_PAD, :] = jnp.zeros((CONV_PAD, n_qkv), F32)

    for n in range(0, n_qkv, 256):
        ext_ref[CONV_PAD:CONV_PAD + tm, n:n + 256] = jnp.dot(
            xb, wqkv_ref[:, n:n + 256], preferred_element_type=F32)

    for n in range(0, n_qkv, LANES):
        w = conv_ref[:, n:n + LANES]
        y = w[3:4, :] * ext_ref[CONV_PAD:CONV_PAD + tm, n:n + LANES]
        for j in range(1, CONV_WIDTH):
            y = y + w[3 - j:4 - j, :] * ext_ref[CONV_PAD - j:CONV_PAD - j + tm, n:n + LANES]
        y = _silu(y)
        if n < 2 * dn_dim:
            y = y * lax.rsqrt(jnp.sum(y * y, axis=-1, keepdims=True) + NORM_EPS)
            if n < dn_dim:
                y = y * (DN_HEAD_DIM ** -0.5)
        qkv_ref[:, n:n + LANES] = y.astype(BF16)

    ext_ref[0:CONV_PAD, :] = ext_ref[tm:tm + CONV_PAD, :]

    for n in range(0, dn_dim, 256):
        z = jnp.dot(xb, wz_ref[:, n:n + 256], preferred_element_type=F32)
        sz_ref[:, n:n + 256] = _silu(z).astype(BF16)

    ab = jnp.dot(xb, wab_ref[...], preferred_element_type=F32)
    neg_exp_alog = gpar_ref[0:1, :]
    dt_bias = gpar_ref[1:2, :]
    sp_in = ab + dt_bias
    softplus = jnp.maximum(sp_in, 0.0) + jnp.log1p(jnp.exp(-jnp.abs(sp_in)))
    lane = lax.broadcasted_iota(jnp.int32, ab.shape, 1)
    gb_ref[...] = jnp.where(lane < DN_HEADS, neg_exp_alog * softplus, _sigmoid(ab))


def _proj_dn(x2, wqkv, wz, wab, conv_w, gpar, *, bsz, seq, tm):
    d = x2.shape[1]
    dn_dim = wz.shape[1]
    nl = seq // tm
    tok = lambda b, i: (b * nl + i, 0)
    return pl.pallas_call(
        functools.partial(_proj_dn_kernel, tm=tm, dn_dim=dn_dim),
        grid=(bsz, nl),
        in_specs=[pl.BlockSpec((tm, d), tok),
                  _const_spec(wqkv.shape), _const_spec(wz.shape), _const_spec(wab.shape),
                  _const_spec(conv_w.shape), _const_spec(gpar.shape)],
        out_specs=[pl.BlockSpec((tm, 3 * dn_dim), tok),
                   pl.BlockSpec((tm, dn_dim), tok),
                   pl.BlockSpec((tm, LANES), tok)],
        out_shape=[jax.ShapeDtypeStruct((bsz * seq, 3 * dn_dim), BF16),
                   jax.ShapeDtypeStruct((bsz * seq, dn_dim), BF16),
                   jax.ShapeDtypeStruct((bsz * seq, LANES), F32)],
        scratch_shapes=[pltpu.VMEM((tm + CONV_PAD, 3 * dn_dim), F32)],
        compiler_params=_params(2),
        name="proj_dn",
    )(x2, wqkv, wz, wab, conv_w, gpar)


def _delta_kernel(q_ref, k_ref, v_ref, gb_ref, sz_ref, ng_ref, o_ref, state_ref, *, tc):
    c = DELTA_CHUNK
    hd = DN_HEAD_DIM
    pw = 2 * hd
    assert c == hd

    @pl.when(pl.program_id(1) == 0)
    def _():
        state_ref[...] = jnp.zeros(state_ref.shape, F32)

    row = lax.broadcasted_iota(jnp.int32, (c, pw), 0)
    lane = lax.broadcasted_iota(jnp.int32, (c, pw), 1)
    first = lane < hd
    col = jnp.where(first, lane, lane - hd)
    causal = row >= col
    strict = row > col
    row_s = lax.broadcasted_iota(jnp.int32, (pw, pw), 0)
    lane_s = lax.broadcasted_iota(jnp.int32, (pw, pw), 1)
    same_head = (row_s < hd) == (lane_s < hd)
    trow = lax.broadcasted_iota(jnp.int32, (c, c), 0)
    tcol = lax.broadcasted_iota(jnp.int32, (c, c), 1)
    tri = (trow >= tcol).astype(F32)
    norm_g = jnp.concatenate([ng_ref[...], ng_ref[...]], axis=1)

    def blockdiag(x):
        z = jnp.zeros_like(x)
        return jnp.concatenate([jnp.where(first, x, z), jnp.where(first, z, x)], axis=0)

    def pair_cols(a, j0, j1):
        return jnp.concatenate([jnp.broadcast_to(a[:, j0:j0 + 1], (c, hd)),
                                jnp.broadcast_to(a[:, j1:j1 + 1], (c, hd))], axis=1)

    def pair_rows(a, j0, j1, r):
        return jnp.concatenate([jnp.broadcast_to(a[r:r + 1, j0:j0 + 1], (1, hd)),
                                jnp.broadcast_to(a[r:r + 1, j1:j1 + 1], (1, hd))], axis=1)

    n_pairs = DN_HEADS // 2
    n_chunks = tc // c
    units = [(ci, p) for ci in range(n_chunks) for p in range(n_pairs)]

    gcums = []
    for ci in range(n_chunks):
        gb = gb_ref[ci * c:(ci + 1) * c, :]
        gcum = _dot_f32(tri, gb)
        gcums.append((gb, gcum, gcum.T))
    pre = {}
    for ci, p in units:
        gb, gcum, gcum_t = gcums[ci]
        h0, h1 = 2 * p, 2 * p + 1
        rows = slice(ci * c, (ci + 1) * c)
        cols = slice(p * pw, (p + 1) * pw)
        q = q_ref[rows, cols].astype(F32)
        k_b = k_ref[rows, cols]
        k = k_b.astype(F32)
        v = v_ref[rows, cols].astype(F32)
        g_col = pair_cols(gcum, h0, h1)
        g_row = jnp.concatenate([gcum_t[h0:h0 + 1, :], gcum_t[h1:h1 + 1, :]], axis=1)
        g_last = pair_rows(gcum, h0, h1, c - 1)
        beta = pair_cols(gb, DN_HEADS + h0, DN_HEADS + h1)
        decay = jnp.where(causal, jnp.exp(jnp.where(causal, g_col - g_row, 0.0)), 0.0)
        exp_g = jnp.exp(g_col)
        k_beta = k * beta
        aq = lax.dot_general(jnp.concatenate([k_beta, q], axis=0).astype(BF16), blockdiag(k_b),
                             NT_DIMS, preferred_element_type=F32)
        pre[ci, p] = dict(
            neg_a=jnp.where(strict, -(aq[:c] * decay), 0.0),
            qk=(aq[c:] * decay).astype(BF16),
            vb_bd=blockdiag((v * beta).astype(BF16)),
            kbe_bd=blockdiag((k_beta * exp_g).astype(BF16)),
            q_exp=(q * exp_g).astype(BF16),
            k_dec=(k * jnp.exp(g_last - g_col)).astype(BF16),
            exp_last=jnp.exp(g_last))

    base = INVERSE_BASE_BLOCK

    def masks(b):
        row_b = lax.broadcasted_iota(jnp.int32, (b, pw), 0)
        lane_b = lax.broadcasted_iota(jnp.int32, (b, pw), 1)
        blk_id = lax.shift_right_logical(lane_b, b.bit_length() - 1)
        blocks = [blk_id == r for r in range(pw // b)]
        eye_b = (row_b == (lane_b & (b - 1))).astype(F32)
        return blocks, eye_b

    def lane_blockdiag(x, blocks):
        z = jnp.zeros_like(x)
        return jnp.concatenate([jnp.where(m, x, z) for m in blocks], axis=0)

    def bd_parts(x, blocks):
        hi, lo = _split(x)
        return lane_blockdiag(hi, blocks), lane_blockdiag(lo, blocks)

    def diag_blocks(a, b):
        lane_2b = lax.broadcasted_iota(jnp.int32, (b, pw), 1)
        upper = (lane_2b & (2 * b - 1)) < b
        return jnp.where(upper, a[:b], a[b:])

    packed = {c: {u: pre[u]["neg_a"] for u in units}}
    size = c
    while size > base:
        packed[size // 2] = {u: diag_blocks(packed[size][u], size // 2) for u in units}
        size //= 2

    blocks, eye_b = masks(base)
    nil = dict(packed[base])
    t_blk = {u: eye_b + nil[u] for u in units}
    for u in units:
        nil[u], = _dot3_rows([nil[u]], *bd_parts(nil[u], blocks))
    levels = base.bit_length() - 2
    for lvl in range(levels):
        rhs = {u: bd_parts(nil[u], blocks) for u in units}
        for u in units:
            if lvl + 1 < levels:
                t_inc, nil[u] = _dot3_rows([t_blk[u], nil[u]], *rhs[u])
            else:
                t_inc, = _dot3_rows([t_blk[u]], *rhs[u])
            t_blk[u] = t_blk[u] + t_inc

    size = base
    while size < c:
        blocks, _ = masks(size)
        lane_s = lax.broadcasted_iota(jnp.int32, (size, pw), 1)
        left = (lane_s & (2 * size - 1)) < size
        x21 = {}
        for u in units:
            n21 = jnp.where(left, packed[2 * size][u][size:], 0.0)
            x21[u], = _dot3_rows([n21], *bd_parts(t_blk[u], blocks))
        grown = {}
        for u in units:
            x_hi, x_lo = _split(x21[u])
            z = jnp.zeros_like(x_hi)
            rhs_hi = jnp.concatenate([jnp.where(blocks[r - 1], x_hi, z) if r % 2 else z
                                      for r in range(pw // size)], axis=0)
            rhs_lo = jnp.concatenate([jnp.where(blocks[r - 1], x_lo, z) if r % 2 else z
                                      for r in range(pw // size)], axis=0)
            t21, = _dot3_rows([t_blk[u]], rhs_hi, rhs_lo)
            grown[u] = jnp.concatenate([jnp.where(left, t_blk[u], 0.0),
                                        jnp.where(left, t21, t_blk[u])], axis=0)
        t_blk = grown
        size *= 2
    t_inv = t_blk
    u_mat, w_mat = {}, {}
    for u in units:
        t_b = t_inv[u].astype(BF16)
        u_mat[u] = jnp.dot(t_b, pre[u]["vb_bd"], preferred_element_type=F32)
        w_mat[u] = jnp.dot(t_b, pre[u]["kbe_bd"], preferred_element_type=F32).astype(BF16)

    states = [state_ref[p] for p in range(n_pairs)]
    for ci in range(n_chunks):
        rows = slice(ci * c, (ci + 1) * c)
        for p in range(n_pairs):
            u = (ci, p)
            cols = slice(p * pw, (p + 1) * pw)
            state = states[p]
            ws_qs = jnp.dot(jnp.concatenate([w_mat[u], pre[u]["q_exp"]], axis=0), state.astype(BF16),
                            preferred_element_type=F32)
            v_new_b = (u_mat[u] - ws_qs[:c]).astype(BF16)
            out = ws_qs[c:] + jnp.dot(pre[u]["qk"], blockdiag(v_new_b), preferred_element_type=F32)
            kv = lax.dot_general(pre[u]["k_dec"], v_new_b, TN_DIMS, preferred_element_type=F32)
            states[p] = jnp.where(same_head, state * pre[u]["exp_last"] + kv, 0.0)
            sq = out * out
            ms = jnp.concatenate(
                [jnp.broadcast_to(jnp.mean(sq[:, :hd], axis=-1, keepdims=True), (c, hd)),
                 jnp.broadcast_to(jnp.mean(sq[:, hd:], axis=-1, keepdims=True), (c, hd))], axis=1)
            out = out * lax.rsqrt(ms + NORM_EPS) * norm_g * sz_ref[rows, cols].astype(F32)
            o_ref[rows, cols] = out.astype(BF16)
    for p in range(n_pairs):
        state_ref[p] = states[p]


def _delta(qkv, sz, gb, norm_g, *, bsz, seq, tc):
    dn_dim = sz.shape[1]
    nl = seq // tc
    tok = lambda b, i: (b * nl + i, 0)
    return pl.pallas_call(
        functools.partial(_delta_kernel, tc=tc),
        grid=(bsz, nl),
        in_specs=[pl.BlockSpec((tc, dn_dim), lambda b, i: (b * nl + i, 0)),
                  pl.BlockSpec((tc, dn_dim), lambda b, i: (b * nl + i, 1)),
                  pl.BlockSpec((tc, dn_dim), lambda b, i: (b * nl + i, 2)),
                  pl.BlockSpec((tc, LANES), tok),
                  pl.BlockSpec((tc, dn_dim), tok),
                  _const_spec(norm_g.shape)],
        out_specs=pl.BlockSpec((tc, dn_dim), tok),
        out_shape=jax.ShapeDtypeStruct((bsz * seq, dn_dim), BF16),
        scratch_shapes=[pltpu.VMEM((DN_HEADS // 2, 2 * DN_HEAD_DIM, 2 * DN_HEAD_DIM), F32)],
        compiler_params=_params(2),
        name="delta",
    )(qkv, qkv, qkv, gb, sz, norm_g)


def _sgu_body(x_ref, wuv_ref, wgate_ref, lng_ref, lnb_ref, sw_ref, sb_ref, wso_ref, u_ref, v_ref, gated_ref,
              *, tm, sgu_dim, d_model):
    xb = x_ref[...].astype(BF16)
    inv_sqrt2 = 2.0 ** -0.5
    for n in range(0, 2 * sgu_dim, 256):
        uv = jnp.dot(xb, wuv_ref[:, n:n + 256], preferred_element_type=F32)
        act = 0.5 * uv * (1.0 + lax.erf(uv * inv_sqrt2))
        if n < sgu_dim:
            u_ref[:, n:n + 256] = act
        else:
            v_ref[:, n - sgu_dim:n - sgu_dim + 256] = act

    v = v_ref[...]
    vc = v - jnp.mean(v, axis=-1, keepdims=True)
    var = jnp.mean(vc * vc, axis=-1, keepdims=True)
    v_ref[...] = vc * lax.rsqrt(var + LN_EPS) * lng_ref[...] + lnb_ref[...]

    row = lax.broadcasted_iota(jnp.int32, (SGU_CHUNK, SGU_CHUNK), 0)
    col = lax.broadcasted_iota(jnp.int32, (SGU_CHUNK, SGU_CHUNK), 1)
    causal = row >= col
    for g in range(SGU_GROUPS):
        cols = slice(g * LANES, (g + 1) * LANES)
        w_causal = jnp.where(causal, sw_ref[g], 0.0).astype(BF16)
        bias = sb_ref[:, g:g + 1]
        for ci in range(tm // SGU_CHUNK):
            rows = slice(ci * SGU_CHUNK, (ci + 1) * SGU_CHUNK)
            mixed = jnp.dot(w_causal, v_ref[rows, cols].astype(BF16), preferred_element_type=F32) + bias
            gated_ref[rows, cols] = (u_ref[rows, cols] * mixed).astype(BF16)

    gated = gated_ref[...]
    ys, sgd = [], []
    for n in range(0, d_model, 256):
        y_sgu = jnp.dot(gated, wso_ref[:, n:n + 256], preferred_element_type=F32)
        gate_sgu = jnp.dot(xb, wgate_ref[:, d_model + n:d_model + n + 256], preferred_element_type=F32)
        ys.append(_sigmoid(gate_sgu) * y_sgu)
        gate_dn = jnp.dot(xb, wgate_ref[:, n:n + 256], preferred_element_type=F32)
        sgd.append(_sigmoid(gate_dn))
    return jnp.concatenate(ys, axis=1), jnp.concatenate(sgd, axis=1)


def _layer_norm(r, g, b):
    rc = r - jnp.mean(r, axis=-1, keepdims=True)
    var = jnp.mean(rc * rc, axis=-1, keepdims=True)
    return rc * lax.rsqrt(var + LN_EPS) * g + b


def _merge_kernel(x_ref, o_ref, wuv_ref, wgate_ref, lng_ref, lnb_ref, sw_ref, sb_ref, wso_ref,
                  wdn_ref, wout_ref, g_ref, b_ref, wr_ref, br_ref,
                  h_ref, rinfo_ref, dest_ref, nsub_ref, info_ref, u_ref, v_ref, gated_ref,
                  *, alpha, tm, sgu_dim, d_model):
    ys, sgd = _sgu_body(x_ref, wuv_ref, wgate_ref, lng_ref, lnb_ref, sw_ref, sb_ref, wso_ref,
                        u_ref, v_ref, gated_ref, tm=tm, sgu_dim=sgu_dim, d_model=d_model)
    y_dn = jnp.dot(o_ref[...], wdn_ref[...], preferred_element_type=F32)
    y = sgd * y_dn + ys
    mix = jnp.dot(y.astype(BF16), wout_ref[...], preferred_element_type=F32)
    h = _layer_norm(alpha * x_ref[...] + mix, g_ref[...], b_ref[...])
    h_ref[...] = h

    h_hi, h_lo = _split(h)
    parts = jnp.dot(jnp.concatenate([h_hi, h_lo], axis=0), wr_ref[...], preferred_element_type=F32)
    logits = (parts[:tm, :LANES] + parts[tm:, :LANES]) + (parts[:tm, LANES:] + parts[tm:, LANES:]) + br_ref[...]
    lt = logits.T
    row_of = lambda i: lt[i:i + 1, :]
    g_logit = [row_of(g) for g in range(N_GROUPS)]
    g_max = functools.reduce(jnp.maximum, g_logit)
    g_idx = jnp.full(g_max.shape, N_GROUPS - 1, jnp.int32)
    for g in range(N_GROUPS - 2, -1, -1):
        g_idx = jnp.where(g_logit[g] == g_max, g, g_idx)
    group_p = 1.0 / functools.reduce(jnp.add, [jnp.exp(gl - g_max) for gl in g_logit])
    e_logit = []
    for j in range(EXPERTS_PER_GROUP):
        sel = row_of(N_GROUPS + (N_GROUPS - 1) * EXPERTS_PER_GROUP + j)
        for g in range(N_GROUPS - 2, -1, -1):
            sel = jnp.where(g_idx == g, row_of(N_GROUPS + g * EXPERTS_PER_GROUP + j), sel)
        e_logit.append(sel)
    e1 = functools.reduce(jnp.maximum, e_logit)
    j1 = jnp.full(e1.shape, EXPERTS_PER_GROUP - 1, jnp.int32)
    for j in range(EXPERTS_PER_GROUP - 2, -1, -1):
        j1 = jnp.where(e_logit[j] == e1, j, j1)
    rest = [jnp.where(j1 == j, -jnp.inf, e_logit[j]) for j in range(EXPERTS_PER_GROUP)]
    e2 = functools.reduce(jnp.maximum, rest)
    j2 = jnp.full(e2.shape, EXPERTS_PER_GROUP - 1, jnp.int32)
    for j in range(EXPERTS_PER_GROUP - 2, -1, -1):
        j2 = jnp.where(rest[j] == e2, j, j2)
    r = jnp.exp(e2 - e1)
    p1 = group_p / (1.0 + r)
    p2 = group_p * r / (1.0 + r)
    p1_hi = p1.astype(BF16).astype(F32)
    p2_hi = p2.astype(BF16).astype(F32)

    onehot = [jnp.where(g_idx == g, 1.0, 0.0) for g in range(N_GROUPS)]
    trow = lax.broadcasted_iota(jnp.int32, (tm, tm), 0)
    tcol = lax.broadcasted_iota(jnp.int32, (tm, tm), 1)
    earlier = jnp.where(trow < tcol, 1.0, 0.0).astype(BF16)
    onehot_rows = jnp.concatenate(onehot + [jnp.zeros((8 - N_GROUPS, tm), F32)], axis=0).astype(BF16)
    rank = jnp.dot(onehot_rows, earlier, preferred_element_type=F32)
    lane1 = lax.broadcasted_iota(jnp.int32, (1, LANES), 1)
    nsub_row = jnp.zeros((1, LANES), F32)
    dest_row = jnp.zeros((1, tm), F32)
    start = jnp.zeros((1, 1), F32)
    for g in range(N_GROUPS):
        count = jnp.sum(onehot[g], axis=-1, keepdims=True)
        n_sub = jnp.floor((count + (SUB_ROWS - 1)) * (1.0 / SUB_ROWS))
        dest_row = dest_row + onehot[g] * (start * SUB_ROWS + rank[g:g + 1, :])
        nsub_row = jnp.where(lane1 == g, n_sub, jnp.where(lane1 == N_GROUPS + g, start, nsub_row))
        start = start + n_sub
    dest_ref[0] = dest_row.astype(jnp.int32)
    nsub_ref[0] = nsub_row.astype(jnp.int32)
    info_ref[...] = jnp.zeros(info_ref.shape, F32)
    for j in range(EXPERTS_PER_GROUP):
        info_ref[j:j + 1, :] = jnp.where(j1 == j, p1_hi, jnp.where(j2 == j, p2_hi, 0.0))
        info_ref[EXPERTS_PER_GROUP + j:EXPERTS_PER_GROUP + j + 1, :] = jnp.where(
            j1 == j, p1 - p1_hi, jnp.where(j2 == j, p2 - p2_hi, 0.0))
    info_ref[2 * EXPERTS_PER_GROUP:2 * EXPERTS_PER_GROUP + 1, :] = dest_row
    rinfo_ref[...] = info_ref[...].T


def _merge(x2, o, wuv, wgate, sgu_ln_g, sgu_ln_b, spatial_w, spatial_b_t, wso, wdn, wout, ln_g, ln_b, wr, br,
           *, tm, alpha):
    t, d = x2.shape
    sgu_dim = wso.shape[0]
    nb = t // tm
    tok = lambda i: (i, 0)
    consts = (wuv, wgate, sgu_ln_g, sgu_ln_b, spatial_w, spatial_b_t, wso, wdn, wout, ln_g, ln_b, wr, br)
    return pl.pallas_call(
        functools.partial(_merge_kernel, alpha=alpha, tm=tm, sgu_dim=sgu_dim, d_model=d),
        grid=(nb,),
        in_specs=[pl.BlockSpec((tm, d), tok), pl.BlockSpec((tm, o.shape[1]), tok)]
        + [_const_spec(c.shape) for c in consts],
        out_specs=[pl.BlockSpec((tm, d), tok), pl.BlockSpec((tm, LANES), tok),
                   pl.BlockSpec((1, 1, tm), lambda i: (i, 0, 0)),
                   pl.BlockSpec((1, 1, LANES), lambda i: (i, 0, 0))],
        out_shape=[jax.ShapeDtypeStruct((t, d), F32), jax.ShapeDtypeStruct((t, LANES), F32),
                   jax.ShapeDtypeStruct((nb, 1, tm), jnp.int32),
                   jax.ShapeDtypeStruct((nb, 1, LANES), jnp.int32)],
        scratch_shapes=[pltpu.VMEM((LANES, tm), F32),
                        pltpu.VMEM((tm, sgu_dim), F32), pltpu.VMEM((tm, sgu_dim), F32),
                        pltpu.VMEM((tm, sgu_dim), BF16)],
        compiler_params=_params(1),
        name="sgu_merge",
    )(x2, o, *consts)


def _moe_kernel(nsub_ref, h_ref, rinfo_ref, dest_ref, wg_ref, wu_ref, wd_ref, g_ref, b_ref, out_ref, acc_ref,
                *, alpha, tm):
    blk = pl.program_id(0)
    h = h_ref[...]
    hb = h.astype(BF16)
    rinfo = rinfo_ref[...]
    lane = lax.broadcasted_iota(jnp.int32, rinfo.shape, 1)
    cw_parts = jnp.where(lane < 2 * EXPERTS_PER_GROUP, rinfo, 0.0).astype(BF16)
    dest_col = rinfo[:, 2 * EXPERTS_PER_GROUP:2 * EXPERTS_PER_GROUP + 1].astype(jnp.int32)
    dest_row = dest_ref[0]
    sub_row = lax.broadcasted_iota(jnp.int32, (SUB_ROWS, tm), 0)
    sub_lane = lax.broadcasted_iota(jnp.int32, (tm, SUB_ROWS), 1)

    def sub_tiles(groups, first_rows):
        n = len(groups)
        gather = [jnp.where(sub_row + r == dest_row, 1.0, 0.0).astype(BF16) for r in first_rows]
        scatter = [jnp.where(sub_lane + r == dest_col, 1.0, 0.0).astype(BF16) for r in first_rows]
        xs = [jnp.dot(gather[i], hb, preferred_element_type=F32).astype(BF16) for i in range(n)]
        cws = [jnp.dot(gather[i], cw_parts, preferred_element_type=F32) for i in range(n)]
        y = [None] * n
        for j in range(EXPERTS_PER_GROUP):
            hg = [jnp.dot(xs[i], wg_ref[groups[i] * EXPERTS_PER_GROUP + j], preferred_element_type=F32)
                  for i in range(n)]
            hu = [jnp.dot(xs[i], wu_ref[groups[i] * EXPERTS_PER_GROUP + j], preferred_element_type=F32)
                  for i in range(n)]
            for i in range(n):
                c_e = cws[i][:, j:j + 1] + cws[i][:, EXPERTS_PER_GROUP + j:EXPERTS_PER_GROUP + j + 1]
                hm = (_silu(hg[i]) * hu[i] * c_e).astype(BF16)
                part = jnp.dot(hm, wd_ref[groups[i] * EXPERTS_PER_GROUP + j], preferred_element_type=F32)
                y[i] = part if y[i] is None else y[i] + part
        out = [jnp.dot(scatter[i], y[i].astype(BF16), preferred_element_type=F32) for i in range(n)]
        return functools.reduce(jnp.add, out)

    n_sub = [nsub_ref[blk * LANES + g] for g in range(N_GROUPS)]
    first = [nsub_ref[blk * LANES + N_GROUPS + g] for g in range(N_GROUPS)]
    past_end = (tm // SUB_ROWS + N_GROUPS) * SUB_ROWS
    acc_ref[...] = sub_tiles(list(range(N_GROUPS)),
                             [jnp.where(n_sub[g] > 0, first[g] * SUB_ROWS, past_end) for g in range(N_GROUPS)])

    for g in range(N_GROUPS):
        def later_sub_tile(s, carry, g=g):
            acc_ref[...] += sub_tiles([g], [(first[g] + s) * SUB_ROWS])
            return carry

        lax.fori_loop(1, n_sub[g], later_sub_tile, 0)

    out_ref[...] = _layer_norm(alpha * h + acc_ref[...], g_ref[...], b_ref[...])


def _moe(h1, rinfo, dest, nsub, wg, wu, wd, ln_g, ln_b, *, tm, alpha):
    t, d = h1.shape
    nd = lambda shape: pl.BlockSpec(shape, lambda i, ns: (0,) * len(shape), pipeline_mode=pl.Buffered(1))
    tok = lambda i, ns: (i, 0)
    grid_spec = pltpu.PrefetchScalarGridSpec(
        num_scalar_prefetch=1,
        grid=(t // tm,),
        in_specs=[pl.BlockSpec((tm, d), tok), pl.BlockSpec((tm, LANES), tok),
                  pl.BlockSpec((1, 1, tm), lambda i, ns: (i, 0, 0)),
                  nd(wg.shape), nd(wu.shape), nd(wd.shape), nd(ln_g.shape), nd(ln_b.shape)],
        out_specs=pl.BlockSpec((tm, d), tok),
        scratch_shapes=[pltpu.VMEM((tm, d), F32)])
    return pl.pallas_call(
        functools.partial(_moe_kernel, alpha=alpha, tm=tm),
        grid_spec=grid_spec,
        out_shape=jax.ShapeDtypeStruct((t, d), F32),
        compiler_params=_params(1),
        name="moe",
    )(nsub.reshape(-1), h1, rinfo, dest, wg, wu, wd, ln_g, ln_b)


def _pad_lanes(a, width=LANES):
    return jnp.pad(a, [(0, 0)] * (a.ndim - 1) + [(0, width - a.shape[-1])])


def _layer(h, w_in, conv_w, a_log, dt_bias, dn_norm_g, w_dn_out, sgu_ln_g, sgu_ln_b,
           spatial_w, spatial_b, w_sgu_out, w_out, ln1_g, ln1_b,
           router_group_w, router_group_b, router_expert_w, router_expert_b,
           expert_w_gate, expert_w_up, expert_w_down, ln2_g, ln2_b, *, alpha):
    bsz, seq, d = h.shape
    dn_dim = w_dn_out.shape[0]
    sgu_dim = w_sgu_out.shape[0]
    x2 = h.reshape(bsz * seq, d)

    o_z = 3 * dn_dim
    o_a = o_z + dn_dim
    o_uv = o_a + 2 * DN_HEADS
    o_gate = o_uv + 2 * sgu_dim
    wqkv = w_in[:, :o_z].astype(BF16)
    wz = w_in[:, o_z:o_a].astype(BF16)
    wab = _pad_lanes(w_in[:, o_a:o_uv]).astype(BF16)
    wuv = w_in[:, o_uv:o_gate].astype(BF16)
    wgate = w_in[:, o_gate:].astype(BF16)
    gpar = jnp.stack([_pad_lanes(-jnp.exp(a_log.astype(F32))), _pad_lanes(dt_bias.astype(F32))])

    qkv, sz, gb = _proj_dn(x2, wqkv, wz, wab, conv_w, gpar, bsz=bsz, seq=seq, tm=PROJ_TILE)
    o = _delta(qkv, sz, gb, dn_norm_g.reshape(1, -1), bsz=bsz, seq=seq, tc=DELTA_TILE)
    wr = _pad_lanes(jnp.concatenate([router_group_w, router_expert_w], axis=1))
    wr = jnp.concatenate(_split(wr), axis=1)
    br = _pad_lanes(jnp.concatenate([router_group_b, router_expert_b]).reshape(1, -1))
    h1, rinfo, dest, nsub = _merge(x2, o, wuv, wgate, sgu_ln_g.reshape(1, -1), sgu_ln_b.reshape(1, -1),
                                   spatial_w, spatial_b.T, w_sgu_out.astype(BF16),
                                   w_dn_out.astype(BF16), w_out.astype(BF16),
                                   ln1_g.reshape(1, -1), ln1_b.reshape(1, -1), wr, br,
                                   tm=MOE_BLOCK, alpha=alpha)
    h2 = _moe(h1, rinfo, dest, nsub, expert_w_gate.astype(BF16), expert_w_up.astype(BF16),
              expert_w_down.astype(BF16), ln2_g.reshape(1, -1), ln2_b.reshape(1, -1),
              tm=MOE_BLOCK, alpha=alpha)
    return h2.reshape(bsz, seq, d)


def kernel(x, w_in, conv_w, a_log, dt_bias, dn_norm_g, w_dn_out, sgu_ln_g, sgu_ln_b, spatial_w, spatial_b, w_sgu_out, w_out, ln1_g, ln1_b, router_group_w, router_group_b, router_expert_w, router_expert_b, expert_w_gate, expert_w_up, expert_w_down, ln2_g, ln2_b):
    depth = w_in.shape[0]
    alpha = (2.0 * depth) ** 0.25
    h = x
    for l in range(depth):
        h = _layer(h, w_in[l], conv_w[l], a_log[l], dt_bias[l], dn_norm_g[l], w_dn_out[l],
                   sgu_ln_g[l], sgu_ln_b[l], spatial_w[l], spatial_b[l], w_sgu_out[l], w_out[l],
                   ln1_g[l], ln1_b[l], router_group_w[l], router_group_b[l],
                   router_expert_w[l], router_expert_b[l],
                   expert_w_gate[l], expert_w_up[l], expert_w_down[l], ln2_g[l], ln2_b[l], alpha=alpha)
    return h
```

```python
import functools

import jax
import jax.numpy as jnp
from jax import lax
from jax.experimental import pallas as pl
from jax.experimental.pallas import tpu as pltpu

F32 = jnp.float32
BF16 = jnp.bfloat16

LANES = 128
DN_HEADS = 8
DN_HEAD_DIM = 128
CONV_WIDTH = 4
SGU_GROUPS = 8
SGU_CHUNK = 128
N_GROUPS = 4
EXPERTS_PER_GROUP = 4
LN_EPS = 1e-5
NORM_EPS = 1e-6
DELTA_CHUNK = 128
INVERSE_BASE_BLOCK = 32
CONV_PAD = 8
PROJ_TILE = 1024
DELTA_TILE = 1024
MOE_BLOCK = 512
SUB_ROWS = 160
VMEM_LIMIT = 56 * 1024 * 1024

NT_DIMS = (((1,), (1,)), ((), ()))
TN_DIMS = (((0,), (0,)), ((), ()))


def _dot_f32(a, b):
    return jnp.dot(a, b, preferred_element_type=F32, precision=lax.Precision.HIGHEST)


def _split(x):
    hi = x.astype(BF16)
    return hi, (x - hi.astype(F32)).astype(BF16)


def _dot3_rows(lhs, b_hi, b_lo):
    m = lhs[0].shape[0]
    n = len(lhs)
    parts = [_split(a) for a in lhs]
    his = [hi for hi, _ in parts]
    los = [lo for _, lo in parts]
    r_hi = jnp.dot(jnp.concatenate(his + los, axis=0), b_hi, preferred_element_type=F32)
    r_lo = jnp.dot(jnp.concatenate(his, axis=0) if n > 1 else his[0], b_lo, preferred_element_type=F32)
    return [r_hi[i * m:(i + 1) * m] + r_hi[(n + i) * m:(n + i + 1) * m] + r_lo[i * m:(i + 1) * m]
            for i in range(n)]


def _sigmoid(x):
    return 0.5 + 0.5 * jnp.tanh(0.5 * x)


def _silu(x):
    half = 0.5 * x
    return half + half * jnp.tanh(half)


def _const_spec(shape):
    nd = len(shape)
    return pl.BlockSpec(shape, lambda *_: (0,) * nd, pipeline_mode=pl.Buffered(1))


def _params(n_axes):
    return pltpu.CompilerParams(dimension_semantics=("arbitrary",) * n_axes,
                                vmem_limit_bytes=VMEM_LIMIT)


def _proj_dn_kernel(x_ref, wqkv_ref, wz_ref, wab_ref, conv_ref, gpar_ref,
                    qkv_ref, sz_ref, gb_ref, ext_ref, *, tm, dn_dim):
    i = pl.program_id(1)
    xb = x_ref[...].astype(BF16)
    n_qkv = 3 * dn_dim

    @pl.when(i == 0)
    def _():
        ext_ref[0:CONV_PAD, :] = jnp.zeros((CONV_PAD, n_qkv), F32)

    for n in range(0, n_qkv, 256):
        ext_ref[CONV_PAD:CONV_PAD + tm, n:n + 256] = jnp.dot(
            xb, wqkv_ref[:, n:n + 256], preferred_element_type=F32)

    for n in range(0, n_qkv, LANES):
        w = conv_ref[:, n:n + LANES]
        y = w[3:4, :] * ext_ref[CONV_PAD:CONV_PAD + tm, n:n + LANES]
        for j in range(1, CONV_WIDTH):
            y = y + w[3 - j:4 - j, :] * ext_ref[CONV_PAD - j:CONV_PAD - j + tm, n:n + LANES]
        y = _silu(y)
        if n < 2 * dn_dim:
            y = y * lax.rsqrt(jnp.sum(y * y, axis=-1, keepdims=True) + NORM_EPS)
            if n < dn_dim:
                y = y * (DN_HEAD_DIM ** -0.5)
        qkv_ref[:, n:n + LANES] = y.astype(BF16)

    ext_ref[0:CONV_PAD, :] = ext_ref[tm:tm + CONV_PAD, :]

    for n in range(0, dn_dim, 256):
        z = jnp.dot(xb, wz_ref[:, n:n + 256], preferred_element_type=F32)
        sz_ref[:, n:n + 256] = _silu(z).astype(BF16)

    ab = jnp.dot(xb, wab_ref[...], preferred_element_type=F32)
    neg_exp_alog = gpar_ref[0:1, :]
    dt_bias = gpar_ref[1:2, :]
    sp_in = ab + dt_bias
    softplus = jnp.maximum(sp_in, 0.0) + jnp.log1p(jnp.exp(-jnp.abs(sp_in)))
    lane = lax.broadcasted_iota(jnp.int32, ab.shape, 1)
    gb_ref[...] = jnp.where(lane < DN_HEADS, neg_exp_alog * softplus, _sigmoid(ab))


def _proj_dn(x2, wqkv, wz, wab, conv_w, gpar, *, bsz, seq, tm):
    d = x2.shape[1]
    dn_dim = wz.shape[1]
    nl = seq // tm
    tok = lambda b, i: (b * nl + i, 0)
    return pl.pallas_call(
        functools.partial(_proj_dn_kernel, tm=tm, dn_dim=dn_dim),
        grid=(bsz, nl),
        in_specs=[pl.BlockSpec((tm, d), tok),
                  _const_spec(wqkv.shape), _const_spec(wz.shape), _const_spec(wab.shape),
                  _const_spec(conv_w.shape), _const_spec(gpar.shape)],
        out_specs=[pl.BlockSpec((tm, 3 * dn_dim), tok),
                   pl.BlockSpec((tm, dn_dim), tok),
                   pl.BlockSpec((tm, LANES), tok)],
        out_shape=[jax.ShapeDtypeStruct((bsz * seq, 3 * dn_dim), BF16),
                   jax.ShapeDtypeStruct((bsz * seq, dn_dim), BF16),
                   jax.ShapeDtypeStruct((bsz * seq, LANES), F32)],
        scratch_shapes=[pltpu.VMEM((tm + CONV_PAD, 3 * dn_dim), F32)],
        compiler_params=_params(2),
        name="proj_dn",
    )(x2, wqkv, wz, wab, conv_w, gpar)


def _delta_kernel(q_ref, k_ref, v_ref, gb_ref, sz_ref, ng_ref, o_ref, state_ref, *, tc):
    c = DELTA_CHUNK
    hd = DN_HEAD_DIM
    pw = 2 * hd
    assert c == hd

    @pl.when(pl.program_id(1) == 0)
    def _():
        state_ref[...] = jnp.zeros(state_ref.shape, F32)

    row = lax.broadcasted_iota(jnp.int32, (c, pw), 0)
    lane = lax.broadcasted_iota(jnp.int32, (c, pw), 1)
    first = lane < hd
    col = jnp.where(first, lane, lane - hd)
    causal = row >= col
    strict = row > col
    row_s = lax.broadcasted_iota(jnp.int32, (pw, pw), 0)
    lane_s = lax.broadcasted_iota(jnp.int32, (pw, pw), 1)
    same_head = (row_s < hd) == (lane_s < hd)
    trow = lax.broadcasted_iota(jnp.int32, (c, c), 0)
    tcol = lax.broadcasted_iota(jnp.int32, (c, c), 1)
    tri = (trow >= tcol).astype(F32)
    norm_g = jnp.concatenate([ng_ref[...], ng_ref[...]], axis=1)

    def blockdiag(x):
        z = jnp.zeros_like(x)
        return jnp.concatenate([jnp.where(first, x, z), jnp.where(first, z, x)], axis=0)

    def pair_cols(a, j0, j1):
        return jnp.concatenate([jnp.broadcast_to(a[:, j0:j0 + 1], (c, hd)),
                                jnp.broadcast_to(a[:, j1:j1 + 1], (c, hd))], axis=1)

    def pair_rows(a, j0, j1, r):
        return jnp.concatenate([jnp.broadcast_to(a[r:r + 1, j0:j0 + 1], (1, hd)),
                                jnp.broadcast_to(a[r:r + 1, j1:j1 + 1], (1, hd))], axis=1)

    n_pairs = DN_HEADS // 2
    n_chunks = tc // c
    units = [(ci, p) for ci in range(n_chunks) for p in range(n_pairs)]

    gcums = []
    for ci in range(n_chunks):
        gb = gb_ref[ci * c:(ci + 1) * c, :]
        gcum = _dot_f32(tri, gb)
        gcums.append((gb, gcum, gcum.T))
    pre = {}
    for ci, p in units:
        gb, gcum, gcum_t = gcums[ci]
        h0, h1 = 2 * p, 2 * p + 1
        rows = slice(ci * c, (ci + 1) * c)
        cols = slice(p * pw, (p + 1) * pw)
        q = q_ref[rows, cols].astype(F32)
        k_b = k_ref[rows, cols]
        k = k_b.astype(F32)
        v = v_ref[rows, cols].astype(F32)
        g_col = pair_cols(gcum, h0, h1)
        g_row = jnp.concatenate([gcum_t[h0:h0 + 1, :], gcum_t[h1:h1 + 1, :]], axis=1)
        g_last = pair_rows(gcum, h0, h1, c - 1)
        beta = pair_cols(gb, DN_HEADS + h0, DN_HEADS + h1)
        decay = jnp.where(causal, jnp.exp(jnp.where(causal, g_col - g_row, 0.0)), 0.0)
        exp_g = jnp.exp(g_col)
        k_beta = k * beta
        aq = lax.dot_general(jnp.concatenate([k_beta, q], axis=0).astype(BF16), blockdiag(k_b),
                             NT_DIMS, preferred_element_type=F32)
        pre[ci, p] = dict(
            neg_a=jnp.where(strict, -(aq[:c] * decay), 0.0),
            qk=(aq[c:] * decay).astype(BF16),
            vb_bd=blockdiag((v * beta).astype(BF16)),
            kbe_bd=blockdiag((k_beta * exp_g).astype(BF16)),
            q_exp=(q * exp_g).astype(BF16),
            k_dec=(k * jnp.exp(g_last - g_col)).astype(BF16),
            exp_last=jnp.exp(g_last))

    base = INVERSE_BASE_BLOCK

    def masks(b):
        row_b = lax.broadcasted_iota(jnp.int32, (b, pw), 0)
        lane_b = lax.broadcasted_iota(jnp.int32, (b, pw), 1)
        blk_id = lax.shift_right_logical(lane_b, b.bit_length() - 1)
        blocks = [blk_id == r for r in range(pw // b)]
        eye_b = (row_b == (lane_b & (b - 1))).astype(F32)
        return blocks, eye_b

    def lane_blockdiag(x, blocks):
        z = jnp.zeros_like(x)
        return jnp.concatenate([jnp.where(m, x, z) for m in blocks], axis=0)

    def bd_parts(x, blocks):
        hi, lo = _split(x)
        return lane_blockdiag(hi, blocks), lane_blockdiag(lo, blocks)

    def diag_blocks(a, b):
        lane_2b = lax.broadcasted_iota(jnp.int32, (b, pw), 1)
        upper = (lane_2b & (2 * b - 1)) < b
        return jnp.where(upper, a[:b], a[b:])

    packed = {c: {u: pre[u]["neg_a"] for u in units}}
    size = c
    while size > base:
        packed[size // 2] = {u: diag_blocks(packed[size][u], size // 2) for u in units}
        size //= 2

    blocks, eye_b = masks(base)
    nil = dict(packed[base])
    t_blk = {u: eye_b + nil[u] for u in units}
    for u in units:
        nil[u], = _dot3_rows([nil[u]], *bd_parts(nil[u], blocks))
    levels = base.bit_length() - 2
    for lvl in range(levels):
        rhs = {u: bd_parts(nil[u], blocks) for u in units}
        for u in units:
            if lvl + 1 < levels:
                t_inc, nil[u] = _dot3_rows([t_blk[u], nil[u]], *rhs[u])
            else:
                t_inc, = _dot3_rows([t_blk[u]], *rhs[u])
            t_blk[u] = t_blk[u] + t_inc

    size = base
    while size < c:
        blocks, _ = masks(size)
        lane_s = lax.broadcasted_iota(jnp.int32, (size, pw), 1)
        left = (lane_s & (2 * size - 1)) < size
        x21 = {}
        for u in units:
            n21 = jnp.where(left, packed[2 * size][u][size:], 0.0)
            x21[u], = _dot3_rows([n21], *bd_parts(t_blk[u], blocks))
        grown = {}
        for u in units:
            x_hi, x_lo = _split(x21[u])
            z = jnp.zeros_like(x_hi)
            rhs_hi = jnp.concatenate([jnp.where(blocks[r - 1], x_hi, z) if r % 2 else z
                                      for r in range(pw // size)], axis=0)
            rhs_lo = jnp.concatenate([jnp.where(blocks[r - 1], x_lo, z) if r % 2 else z
                                      for r in range(pw // size)], axis=0)
            t21, = _dot3_rows([t_blk[u]], rhs_hi, rhs_lo)
            grown[u] = jnp.concatenate([jnp.where(left, t_blk[u], 0.0),
                                        jnp.where(left, t21, t_blk[u])], axis=0)
        t_blk = grown
        size *= 2
    t_inv = t_blk
    u_mat, w_mat = {}, {}
    for u in units:
        t_b = t_inv[u].astype(BF16)
        u_mat[u] = jnp.dot(t_b, pre[u]["vb_bd"], preferred_element_type=F32)
        w_mat[u] = jnp.dot(t_b, pre[u]["kbe_bd"], preferred_element_type=F32).astype(BF16)

    states = [state_ref[p] for p in range(n_pairs)]
    for ci in range(n_chunks):
        rows = slice(ci * c, (ci + 1) * c)
        for p in range(n_pairs):
            u = (ci, p)
            cols = slice(p * pw, (p + 1) * pw)
            state = states[p]
            ws_qs = jnp.dot(jnp.concatenate([w_mat[u], pre[u]["q_exp"]], axis=0), state.astype(BF16),
                            preferred_element_type=F32)
            v_new_b = (u_mat[u] - ws_qs[:c]).astype(BF16)
            out = ws_qs[c:] + jnp.dot(pre[u]["qk"], blockdiag(v_new_b), preferred_element_type=F32)
            kv = lax.dot_general(pre[u]["k_dec"], v_new_b, TN_DIMS, preferred_element_type=F32)
            states[p] = jnp.where(same_head, state * pre[u]["exp_last"] + kv, 0.0)
            sq = out * out
            ms = jnp.concatenate(
                [jnp.broadcast_to(jnp.mean(sq[:, :hd], axis=-1, keepdims=True), (c, hd)),
                 jnp.broadcast_to(jnp.mean(sq[:, hd:], axis=-1, keepdims=True), (c, hd))], axis=1)
            out = out * lax.rsqrt(ms + NORM_EPS) * norm_g * sz_ref[rows, cols].astype(F32)
            o_ref[rows, cols] = out.astype(BF16)
    for p in range(n_pairs):
        state_ref[p] = states[p]


def _delta(qkv, sz, gb, norm_g, *, bsz, seq, tc):
    dn_dim = sz.shape[1]
    nl = seq // tc
    tok = lambda b, i: (b * nl + i, 0)
    return pl.pallas_call(
        functools.partial(_delta_kernel, tc=tc),
        grid=(bsz, nl),
        in_specs=[pl.BlockSpec((tc, dn_dim), lambda b, i: (b * nl + i, 0)),
                  pl.BlockSpec((tc, dn_dim), lambda b, i: (b * nl + i, 1)),
                  pl.BlockSpec((tc, dn_dim), lambda b, i: (b * nl + i, 2)),
                  pl.BlockSpec((tc, LANES), tok),
                  pl.BlockSpec((tc, dn_dim), tok),
                  _const_spec(norm_g.shape)],
        out_specs=pl.BlockSpec((tc, dn_dim), tok),
        out_shape=jax.ShapeDtypeStruct((bsz * seq, dn_dim), BF16),
        scratch_shapes=[pltpu.VMEM((DN_HEADS // 2, 2 * DN_HEAD_DIM, 2 * DN_HEAD_DIM), F32)],
        compiler_params=_params(2),
        name="delta",
    )(qkv, qkv, qkv, gb, sz, norm_g)


def _sgu_body(x_ref, wuv_ref, wgate_ref, lng_ref, lnb_ref, sw_ref, sb_ref, wso_ref, u_ref, v_ref, gated_ref,
              *, tm, sgu_dim, d_model):
    xb = x_ref[...].astype(BF16)
    inv_sqrt2 = 2.0 ** -0.5
    for n in range(0, 2 * sgu_dim, 256):
        uv = jnp.dot(xb, wuv_ref[:, n:n + 256], preferred_element_type=F32)
        act = 0.5 * uv * (1.0 + lax.erf(uv * inv_sqrt2))
        if n < sgu_dim:
            u_ref[:, n:n + 256] = act
        else:
            v_ref[:, n - sgu_dim:n - sgu_dim + 256] = act

    v = v_ref[...]
    vc = v - jnp.mean(v, axis=-1, keepdims=True)
    var = jnp.mean(vc * vc, axis=-1, keepdims=True)
    v_ref[...] = vc * lax.rsqrt(var + LN_EPS) * lng_ref[...] + lnb_ref[...]

    row = lax.broadcasted_iota(jnp.int32, (SGU_CHUNK, SGU_CHUNK), 0)
    col = lax.broadcasted_iota(jnp.int32, (SGU_CHUNK, SGU_CHUNK), 1)
    causal = row >= col
    for g in range(SGU_GROUPS):
        cols = slice(g * LANES, (g + 1) * LANES)
        w_causal = jnp.where(causal, sw_ref[g], 0.0).astype(BF16)
        bias = sb_ref[:, g:g + 1]
        for ci in range(tm // SGU_CHUNK):
            rows = slice(ci * SGU_CHUNK, (ci + 1) * SGU_CHUNK)
            mixed = jnp.dot(w_causal, v_ref[rows, cols].astype(BF16), preferred_element_type=F32) + bias
            gated_ref[rows, cols] = (u_ref[rows, cols] * mixed).astype(BF16)

    gated = gated_ref[...]
    ys, sgd = [], []
    for n in range(0, d_model, 256):
        y_sgu = jnp.dot(gated, wso_ref[:, n:n + 256], preferred_element_type=F32)
        gate_sgu = jnp.dot(xb, wgate_ref[:, d_model + n:d_model + n + 256], preferred_element_type=F32)
        ys.append(_sigmoid(gate_sgu) * y_sgu)
        gate_dn = jnp.dot(xb, wgate_ref[:, n:n + 256], preferred_element_type=F32)
        sgd.append(_sigmoid(gate_dn))
    return jnp.concatenate(ys, axis=1), jnp.concatenate(sgd, axis=1)


def _layer_norm(r, g, b):
    rc = r - jnp.mean(r, axis=-1, keepdims=True)
    var = jnp.mean(rc * rc, axis=-1, keepdims=True)
    return rc * lax.rsqrt(var + LN_EPS) * g + b


def _merge_kernel(x_ref, o_ref, wuv_ref, wgate_ref, lng_ref, lnb_ref, sw_ref, sb_ref, wso_ref,
                  wdn_ref, wout_ref, g_ref, b_ref, wr_ref, br_ref,
                  h_ref, rinfo_ref, dest_ref, nsub_ref, info_ref, u_ref, v_ref, gated_ref,
                  *, alpha, tm, sgu_dim, d_model):
    ys, sgd = _sgu_body(x_ref, wuv_ref, wgate_ref, lng_ref, lnb_ref, sw_ref, sb_ref, wso_ref,
                        u_ref, v_ref, gated_ref, tm=tm, sgu_dim=sgu_dim, d_model=d_model)
    y_dn = jnp.dot(o_ref[...], wdn_ref[...], preferred_element_type=F32)
    y = sgd * y_dn + ys
    mix = jnp.dot(y.astype(BF16), wout_ref[...], preferred_element_type=F32)
    h = _layer_norm(alpha * x_ref[...] + mix, g_ref[...], b_ref[...])
    h_ref[...] = h

    h_hi, h_lo = _split(h)
    parts = jnp.dot(jnp.concatenate([h_hi, h_lo], axis=0), wr_ref[...], preferred_element_type=F32)
    logits = (parts[:tm, :LANES] + parts[tm:, :LANES]) + (parts[:tm, LANES:] + parts[tm:, LANES:]) + br_ref[...]
    lt = logits.T
    row_of = lambda i: lt[i:i + 1, :]
    g_logit = [row_of(g) for g in range(N_GROUPS)]
    g_max = functools.reduce(jnp.maximum, g_logit)
    g_idx = jnp.full(g_max.shape, N_GROUPS - 1, jnp.int32)
    for g in range(N_GROUPS - 2, -1, -1):
        g_idx = jnp.where(g_logit[g] == g_max, g, g_idx)
    group_p = 1.0 / functools.reduce(jnp.add, [jnp.exp(gl - g_max) for gl in g_logit])
    e_logit = []
    for j in range(EXPERTS_PER_GROUP):
        sel = row_of(N_GROUPS + (N_GROUPS - 1) * EXPERTS_PER_GROUP + j)
        for g in range(N_GROUPS - 2, -1, -1):
            sel = jnp.where(g_idx == g, row_of(N_GROUPS + g * EXPERTS_PER_GROUP + j), sel)
        e_logit.append(sel)
    e1 = functools.reduce(jnp.maximum, e_logit)
    j1 = jnp.full(e1.shape, EXPERTS_PER_GROUP - 1, jnp.int32)
    for j in range(EXPERTS_PER_GROUP - 2, -1, -1):
        j1 = jnp.where(e_logit[j] == e1, j, j1)
    rest = [jnp.where(j1 == j, -jnp.inf, e_logit[j]) for j in range(EXPERTS_PER_GROUP)]
    e2 = functools.reduce(jnp.maximum, rest)
    j2 = jnp.full(e2.shape, EXPERTS_PER_GROUP - 1, jnp.int32)
    for j in range(EXPERTS_PER_GROUP - 2, -1, -1):
        j2 = jnp.where(rest[j] == e2, j, j2)
    r = jnp.exp(e2 - e1)
    p1 = group_p / (1.0 + r)
    p2 = group_p * r / (1.0 + r)
    p1_hi = p1.astype(BF16).astype(F32)
    p2_hi = p2.astype(BF16).astype(F32)

    onehot = [jnp.where(g_idx == g, 1.0, 0.0) for g in range(N_GROUPS)]
    trow = lax.broadcasted_iota(jnp.int32, (tm, tm), 0)
    tcol = lax.broadcasted_iota(jnp.int32, (tm, tm), 1)
    earlier = jnp.where(trow < tcol, 1.0, 0.0).astype(BF16)
    onehot_rows = jnp.concatenate(onehot + [jnp.zeros((8 - N_GROUPS, tm), F32)], axis=0).astype(BF16)
    rank = jnp.dot(onehot_rows, earlier, preferred_element_type=F32)
    lane1 = lax.broadcasted_iota(jnp.int32, (1, LANES), 1)
    nsub_row = jnp.zeros((1, LANES), F32)
    dest_row = jnp.zeros((1, tm), F32)
    start = jnp.zeros((1, 1), F32)
    for g in range(N_GROUPS):
        count = jnp.sum(onehot[g], axis=-1, keepdims=True)
        n_sub = jnp.floor((count + (SUB_ROWS - 1)) * (1.0 / SUB_ROWS))
        dest_row = dest_row + onehot[g] * (start * SUB_ROWS + rank[g:g + 1, :])
        nsub_row = jnp.where(lane1 == g, n_sub, jnp.where(lane1 == N_GROUPS + g, start, nsub_row))
        start = start + n_sub
    dest_ref[0] = dest_row.astype(jnp.int32)
    nsub_ref[0] = nsub_row.astype(jnp.int32)
    info_ref[...] = jnp.zeros(info_ref.shape, F32)
    for j in range(EXPERTS_PER_GROUP):
        info_ref[j:j + 1, :] = jnp.where(j1 == j, p1_hi, jnp.where(j2 == j, p2_hi, 0.0))
        info_ref[EXPERTS_PER_GROUP + j:EXPERTS_PER_GROUP + j + 1, :] = jnp.where(
            j1 == j, p1 - p1_hi, jnp.where(j2 == j, p2 - p2_hi, 0.0))
    info_ref[2 * EXPERTS_PER_GROUP:2 * EXPERTS_PER_GROUP + 1, :] = dest_row
    rinfo_ref[...] = info_ref[...].T


def _merge(x2, o, wuv, wgate, sgu_ln_g, sgu_ln_b, spatial_w, spatial_b_t, wso, wdn, wout, ln_g, ln_b, wr, br,
           *, tm, alpha):
    t, d = x2.shape
    sgu_dim = wso.shape[0]
    nb = t // tm
    tok = lambda i: (i, 0)
    consts = (wuv, wgate, sgu_ln_g, sgu_ln_b, spatial_w, spatial_b_t, wso, wdn, wout, ln_g, ln_b, wr, br)
    return pl.pallas_call(
        functools.partial(_merge_kernel, alpha=alpha, tm=tm, sgu_dim=sgu_dim, d_model=d),
        grid=(nb,),
        in_specs=[pl.BlockSpec((tm, d), tok), pl.BlockSpec((tm, o.shape[1]), tok)]
        + [_const_spec(c.shape) for c in consts],
        out_specs=[pl.BlockSpec((tm, d), tok), pl.BlockSpec((tm, LANES), tok),
                   pl.BlockSpec((1, 1, tm), lambda i: (i, 0, 0)),
                   pl.BlockSpec((1, 1, LANES), lambda i: (i, 0, 0))],
        out_shape=[jax.ShapeDtypeStruct((t, d), F32), jax.ShapeDtypeStruct((t, LANES), F32),
                   jax.ShapeDtypeStruct((nb, 1, tm), jnp.int32),
                   jax.ShapeDtypeStruct((nb, 1, LANES), jnp.int32)],
        scratch_shapes=[pltpu.VMEM((LANES, tm), F32),
                        pltpu.VMEM((tm, sgu_dim), F32), pltpu.VMEM((tm, sgu_dim), F32),
                        pltpu.VMEM((tm, sgu_dim), BF16)],
        compiler_params=_params(1),
        name="sgu_merge",
    )(x2, o, *consts)


def _moe_kernel(nsub_ref, h_ref, rinfo_ref, dest_ref, wg_ref, wu_ref, wd_ref, g_ref, b_ref, out_ref, acc_ref,
                *, alpha, tm):
    blk = pl.program_id(0)
    h = h_ref[...]
    hb = h.astype(BF16)
    rinfo = rinfo_ref[...]
    lane = lax.broadcasted_iota(jnp.int32, rinfo.shape, 1)
    cw_parts = jnp.where(lane < 2 * EXPERTS_PER_GROUP, rinfo, 0.0).astype(BF16)
    dest_col = rinfo[:, 2 * EXPERTS_PER_GROUP:2 * EXPERTS_PER_GROUP + 1].astype(jnp.int32)
    dest_row = dest_ref[0]
    sub_row = lax.broadcasted_iota(jnp.int32, (SUB_ROWS, tm), 0)
    sub_lane = lax.broadcasted_iota(jnp.int32, (tm, SUB_ROWS), 1)

    def sub_tiles(groups, first_rows):
        n = len(groups)
        gather = [jnp.where(sub_row + r == dest_row, 1.0, 0.0).astype(BF16) for r in first_rows]
        scatter = [jnp.where(sub_lane + r == dest_col, 1.0, 0.0).astype(BF16) for r in first_rows]
        xs = [jnp.dot(gather[i], hb, preferred_element_type=F32).astype(BF16) for i in range(n)]
        cws = [jnp.dot(gather[i], cw_parts, preferred_element_type=F32) for i in range(n)]
        y = [None] * n
        for j in range(EXPERTS_PER_GROUP):
            hg = [jnp.dot(xs[i], wg_ref[groups[i] * EXPERTS_PER_GROUP + j], preferred_element_type=F32)
                  for i in range(n)]
            hu = [jnp.dot(xs[i], wu_ref[groups[i] * EXPERTS_PER_GROUP + j], preferred_element_type=F32)
                  for i in range(n)]
            for i in range(n):
                c_e = cws[i][:, j:j + 1] + cws[i][:, EXPERTS_PER_GROUP + j:EXPERTS_PER_GROUP + j + 1]
                hm = (_silu(hg[i]) * hu[i] * c_e).astype(BF16)
                part = jnp.dot(hm, wd_ref[groups[i] * EXPERTS_PER_GROUP + j], preferred_element_type=F32)
                y[i] = part if y[i] is None else y[i] + part
        out = [jnp.dot(scatter[i], y[i].astype(BF16), preferred_element_type=F32) for i in range(n)]
        return functools.reduce(jnp.add, out)

    n_sub = [nsub_ref[blk * LANES + g] for g in range(N_GROUPS)]
    first = [nsub_ref[blk * LANES + N_GROUPS + g] for g in range(N_GROUPS)]
    past_end = (tm // SUB_ROWS + N_GROUPS) * SUB_ROWS
    acc_ref[...] = sub_tiles(list(range(N_GROUPS)),
                             [jnp.where(n_sub[g] > 0, first[g] * SUB_ROWS, past_end) for g in range(N_GROUPS)])

    for g in range(N_GROUPS):
        def later_sub_tile(s, carry, g=g):
            acc_ref[...] += sub_tiles([g], [(first[g] + s) * SUB_ROWS])
            return carry

        lax.fori_loop(1, n_sub[g], later_sub_tile, 0)

    out_ref[...] = _layer_norm(alpha * h + acc_ref[...], g_ref[...], b_ref[...])


def _moe(h1, rinfo, dest, nsub, wg, wu, wd, ln_g, ln_b, *, tm, alpha):
    t, d = h1.shape
    nd = lambda shape: pl.BlockSpec(shape, lambda i, ns: (0,) * len(shape), pipeline_mode=pl.Buffered(1))
    tok = lambda i, ns: (i, 0)
    grid_spec = pltpu.PrefetchScalarGridSpec(
        num_scalar_prefetch=1,
        grid=(t // tm,),
        in_specs=[pl.BlockSpec((tm, d), tok), pl.BlockSpec((tm, LANES), tok),
                  pl.BlockSpec((1, 1, tm), lambda i, ns: (i, 0, 0)),
                  nd(wg.shape), nd(wu.shape), nd(wd.shape), nd(ln_g.shape), nd(ln_b.shape)],
        out_specs=pl.BlockSpec((tm, d), tok),
        scratch_shapes=[pltpu.VMEM((tm, d), F32)])
    return pl.pallas_call(
        functools.partial(_moe_kernel, alpha=alpha, tm=tm),
        grid_spec=grid_spec,
        out_shape=jax.ShapeDtypeStruct((t, d), F32),
        compiler_params=_params(1),
        name="moe",
    )(nsub.reshape(-1), h1, rinfo, dest, wg, wu, wd, ln_g, ln_b)


def _pad_lanes(a, width=LANES):
    return jnp.pad(a, [(0, 0)] * (a.ndim - 1) + [(0, width - a.shape[-1])])


def _layer(h, w_in, conv_w, a_log, dt_bias, dn_norm_g, w_dn_out, sgu_ln_g, sgu_ln_b,
           spatial_w, spatial_b, w_sgu_out, w_out, ln1_g, ln1_b,
           router_group_w, router_group_b, router_expert_w, router_expert_b,
           expert_w_gate, expert_w_up, expert_w_down, ln2_g, ln2_b, *, alpha):
    bsz, seq, d = h.shape
    dn_dim = w_dn_out.shape[0]
    sgu_dim = w_sgu_out.shape[0]
    x2 = h.reshape(bsz * seq, d)

    o_z = 3 * dn_dim
    o_a = o_z + dn_dim
    o_uv = o_a + 2 * DN_HEADS
    o_gate = o_uv + 2 * sgu_dim
    wqkv = w_in[:, :o_z].astype(BF16)
    wz = w_in[:, o_z:o_a].astype(BF16)
    wab = _pad_lanes(w_in[:, o_a:o_uv]).astype(BF16)
    wuv = w_in[:, o_uv:o_gate].astype(BF16)
    wgate = w_in[:, o_gate:].astype(BF16)
    gpar = jnp.stack([_pad_lanes(-jnp.exp(a_log.astype(F32))), _pad_lanes(dt_bias.astype(F32))])

    qkv, sz, gb = _proj_dn(x2, wqkv, wz, wab, conv_w, gpar, bsz=bsz, seq=seq, tm=PROJ_TILE)
    o = _delta(qkv, sz, gb, dn_norm_g.reshape(1, -1), bsz=bsz, seq=seq, tc=DELTA_TILE)
    wr = _pad_lanes(jnp.concatenate([router_group_w, router_expert_w], axis=1))
    wr = jnp.concatenate(_split(wr), axis=1)
    br = _pad_lanes(jnp.concatenate([router_group_b, router_expert_b]).reshape(1, -1))
    h1, rinfo, dest, nsub = _merge(x2, o, wuv, wgate, sgu_ln_g.reshape(1, -1), sgu_ln_b.reshape(1, -1),
                                   spatial_w, spatial_b.T, w_sgu_out.astype(BF16),
                                   w_dn_out.astype(BF16), w_out.astype(BF16),
                                   ln1_g.reshape(1, -1), ln1_b.reshape(1, -1), wr, br,
                                   tm=MOE_BLOCK, alpha=alpha)
    h2 = _moe(h1, rinfo, dest, nsub, expert_w_gate.astype(BF16), expert_w_up.astype(BF16),
              expert_w_down.astype(BF16), ln2_g.reshape(1, -1), ln2_b.reshape(1, -1),
              tm=MOE_BLOCK, alpha=alpha)
    return h2.reshape(bsz, seq, d)


def kernel(x, w_in, conv_w, a_log, dt_bias, dn_norm_g, w_dn_out, sgu_ln_g, sgu_ln_b, spatial_w, spatial_b, w_sgu_out, w_out, ln1_g, ln1_b, router_group_w, router_group_b, router_expert_w, router_expert_b, expert_w_gate, expert_w_up, expert_w_down, ln2_g, ln2_b):
    depth = w_in.shape[0]
    alpha = (2.0 * depth) ** 0.25
    h = x
    for l in range(depth):
        h = _layer(h, w_in[l], conv_w[l], a_log[l], dt_bias[l], dn_norm_g[l], w_dn_out[l],
                   sgu_ln_g[l], sgu_ln_b[l], spatial_w[l], spatial_b[l], w_sgu_out[l], w_out[l],
                   ln1_g[l], ln1_b[l], router_group_w[l], router_group_b[l],
                   router_expert_w[l], router_expert_b[l],
                   expert_w_gate[l], expert_w_up[l], expert_w_down[l], ln2_g[l], ln2_b[l], alpha=alpha)
    return h
```

```python
import functools

import jax
import jax.numpy as jnp
from jax import lax
from jax.experimental import pallas as pl
from jax.experimental.pallas import tpu as pltpu

F32 = jnp.float32
BF16 = jnp.bfloat16

LANES = 128
DN_HEADS = 8
DN_HEAD_DIM = 128
CONV_WIDTH = 4
SGU_GROUPS = 8
SGU_CHUNK = 128
N_GROUPS = 4
EXPERTS_PER_GROUP = 4
LN_EPS = 1e-5
NORM_EPS = 1e-6
DELTA_CHUNK = 128
INVERSE_BASE_BLOCK = 32
CONV_PAD = 8
PROJ_TILE = 1024
DELTA_TILE = 512
MOE_BLOCK = 512
SUB_ROWS = 160
VMEM_LIMIT = 56 * 1024 * 1024

NT_DIMS = (((1,), (1,)), ((), ()))
TN_DIMS = (((0,), (0,)), ((), ()))


def _dot_f32(a, b):
    return jnp.dot(a, b, preferred_element_type=F32, precision=lax.Precision.HIGHEST)


def _split(x):
    hi = x.astype(BF16)
    return hi, (x - hi.astype(F32)).astype(BF16)


def _dot3_rows(lhs, b_hi, b_lo):
    m = lhs[0].shape[0]
    n = len(lhs)
    parts = [_split(a) for a in lhs]
    his = [hi for hi, _ in parts]
    los = [lo for _, lo in parts]
    r_hi = jnp.dot(jnp.concatenate(his + los, axis=0), b_hi, preferred_element_type=F32)
    r_lo = jnp.dot(jnp.concatenate(his, axis=0) if n > 1 else his[0], b_lo, preferred_element_type=F32)
    return [r_hi[i * m:(i + 1) * m] + r_hi[(n + i) * m:(n + i + 1) * m] + r_lo[i * m:(i + 1) * m]
            for i in range(n)]


def _sigmoid(x):
    return 0.5 + 0.5 * jnp.tanh(0.5 * x)


def _silu(x):
    half = 0.5 * x
    return half + half * jnp.tanh(half)


def _const_spec(shape):
    nd = len(shape)
    return pl.BlockSpec(shape, lambda *_: (0,) * nd, pipeline_mode=pl.Buffered(1))


def _params(n_axes):
    return pltpu.CompilerParams(dimension_semantics=("arbitrary",) * n_axes,
                                vmem_limit_bytes=VMEM_LIMIT)


def _proj_dn_kernel(x_ref, wqkv_ref, wz_ref, wab_ref, conv_ref, gpar_ref,
                    qkv_ref, sz_ref, gb_ref, ext_ref, *, tm, dn_dim):
    i = pl.program_id(1)
    xb = x_ref[...].astype(BF16)
    n_qkv = 3 * dn_dim

    @pl.when(i == 0)
    def _():
        ext_ref[0:CONV_PAD, :] = jnp.zeros((CONV_PAD, n_qkv), F32)

    for n in range(0, n_qkv, 256):
        ext_ref[CONV_PAD:CONV_PAD + tm, n:n + 256] = jnp.dot(
            xb, wqkv_ref[:, n:n + 256], preferred_element_type=F32)

    for n in range(0, n_qkv, LANES):
        w = conv_ref[:, n:n + LANES]
        y = w[3:4, :] * ext_ref[CONV_PAD:CONV_PAD + tm, n:n + LANES]
        for j in range(1, CONV_WIDTH):
            y = y + w[3 - j:4 - j, :] * ext_ref[CONV_PAD - j:CONV_PAD - j + tm, n:n + LANES]
        y = _silu(y)
        if n < 2 * dn_dim:
            y = y * lax.rsqrt(jnp.sum(y * y, axis=-1, keepdims=True) + NORM_EPS)
            if n < dn_dim:
                y = y * (DN_HEAD_DIM ** -0.5)
        qkv_ref[:, n:n + LANES] = y.astype(BF16)

    ext_ref[0:CONV_PAD, :] = ext_ref[tm:tm + CONV_PAD, :]

    for n in range(0, dn_dim, 256):
        z = jnp.dot(xb, wz_ref[:, n:n + 256], preferred_element_type=F32)
        sz_ref[:, n:n + 256] = _silu(z).astype(BF16)

    ab = jnp.dot(xb, wab_ref[...], preferred_element_type=F32)
    neg_exp_alog = gpar_ref[0:1, :]
    dt_bias = gpar_ref[1:2, :]
    sp_in = ab + dt_bias
    softplus = jnp.maximum(sp_in, 0.0) + jnp.log1p(jnp.exp(-jnp.abs(sp_in)))
    lane = lax.broadcasted_iota(jnp.int32, ab.shape, 1)
    gb_ref[...] = jnp.where(lane < DN_HEADS, neg_exp_alog * softplus, _sigmoid(ab))


def _proj_dn(x2, wqkv, wz, wab, conv_w, gpar, *, bsz, seq, tm):
    d = x2.shape[1]
    dn_dim = wz.shape[1]
    nl = seq // tm
    tok = lambda b, i: (b * nl + i, 0)
    return pl.pallas_call(
        functools.partial(_proj_dn_kernel, tm=tm, dn_dim=dn_dim),
        grid=(bsz, nl),
        in_specs=[pl.BlockSpec((tm, d), tok),
                  _const_spec(wqkv.shape), _const_spec(wz.shape), _const_spec(wab.shape),
                  _const_spec(conv_w.shape), _const_spec(gpar.shape)],
        out_specs=[pl.BlockSpec((tm, 3 * dn_dim), tok),
                   pl.BlockSpec((tm, dn_dim), tok),
                   pl.BlockSpec((tm, LANES), tok)],
        out_shape=[jax.ShapeDtypeStruct((bsz * seq, 3 * dn_dim), BF16),
                   jax.ShapeDtypeStruct((bsz * seq, dn_dim), BF16),
                   jax.ShapeDtypeStruct((bsz * seq, LANES), F32)],
        scratch_shapes=[pltpu.VMEM((tm + CONV_PAD, 3 * dn_dim), F32)],
        compiler_params=_params(2),
        name="proj_dn",
    )(x2, wqkv, wz, wab, conv_w, gpar)


def _delta_kernel(q_ref, k_ref, v_ref, gb_ref, sz_ref, ng_ref, o_ref, state_ref, *, tc):
    c = DELTA_CHUNK
    hd = DN_HEAD_DIM
    pw = 2 * hd
    assert c == hd

    @pl.when(pl.program_id(1) == 0)
    def _():
        state_ref[...] = jnp.zeros(state_ref.shape, F32)

    row = lax.broadcasted_iota(jnp.int32, (c, pw), 0)
    lane = lax.broadcasted_iota(jnp.int32, (c, pw), 1)
    first = lane < hd
    col = jnp.where(first, lane, lane - hd)
    causal = row >= col
    strict = row > col
    row_s = lax.broadcasted_iota(jnp.int32, (pw, pw), 0)
    lane_s = lax.broadcasted_iota(jnp.int32, (pw, pw), 1)
    same_head = (row_s < hd) == (lane_s < hd)
    trow = lax.broadcasted_iota(jnp.int32, (c, c), 0)
    tcol = lax.broadcasted_iota(jnp.int32, (c, c), 1)
    tri = (trow >= tcol).astype(F32)
    norm_g = jnp.concatenate([ng_ref[...], ng_ref[...]], axis=1)

    def blockdiag(x):
        z = jnp.zeros_like(x)
        return jnp.concatenate([jnp.where(first, x, z), jnp.where(first, z, x)], axis=0)

    def pair_cols(a, j0, j1):
        return jnp.concatenate([jnp.broadcast_to(a[:, j0:j0 + 1], (c, hd)),
                                jnp.broadcast_to(a[:, j1:j1 + 1], (c, hd))], axis=1)

    def pair_rows(a, j0, j1, r):
        return jnp.concatenate([jnp.broadcast_to(a[r:r + 1, j0:j0 + 1], (1, hd)),
                                jnp.broadcast_to(a[r:r + 1, j1:j1 + 1], (1, hd))], axis=1)

    n_pairs = DN_HEADS // 2
    n_chunks = tc // c
    units = [(ci, p) for ci in range(n_chunks) for p in range(n_pairs)]

    gcums = []
    for ci in range(n_chunks):
        gb = gb_ref[ci * c:(ci + 1) * c, :]
        gcum = _dot_f32(tri, gb)
        gcums.append((gb, gcum, gcum.T))
    pre = {}
    for ci, p in units:
        gb, gcum, gcum_t = gcums[ci]
        h0, h1 = 2 * p, 2 * p + 1
        rows = slice(ci * c, (ci + 1) * c)
        cols = slice(p * pw, (p + 1) * pw)
        q = q_ref[rows, cols].astype(F32)
        k_b = k_ref[rows, cols]
        k = k_b.astype(F32)
        v = v_ref[rows, cols].astype(F32)
        g_col = pair_cols(gcum, h0, h1)
        g_row = jnp.concatenate([gcum_t[h0:h0 + 1, :], gcum_t[h1:h1 + 1, :]], axis=1)
        g_last = pair_rows(gcum, h0, h1, c - 1)
        beta = pair_cols(gb, DN_HEADS + h0, DN_HEADS + h1)
        decay = jnp.where(causal, jnp.exp(jnp.where(causal, g_col - g_row, 0.0)), 0.0)
        exp_g = jnp.exp(g_col)
        k_beta = k * beta
        aq = lax.dot_general(jnp.concatenate([k_beta, q], axis=0).astype(BF16), blockdiag(k_b),
                             NT_DIMS, preferred_element_type=F32)
        pre[ci, p] = dict(
            neg_a=jnp.where(strict, -(aq[:c] * decay), 0.0),
            qk=(aq[c:] * decay).astype(BF16),
            vb_bd=blockdiag((v * beta).astype(BF16)),
            kbe_bd=blockdiag((k_beta * exp_g).astype(BF16)),
            q_exp=(q * exp_g).astype(BF16),
            k_dec=(k * jnp.exp(g_last - g_col)).astype(BF16),
            exp_last=jnp.exp(g_last))

    base = INVERSE_BASE_BLOCK

    def masks(b):
        row_b = lax.broadcasted_iota(jnp.int32, (b, pw), 0)
        lane_b = lax.broadcasted_iota(jnp.int32, (b, pw), 1)
        blk_id = lax.shift_right_logical(lane_b, b.bit_length() - 1)
        blocks = [blk_id == r for r in range(pw // b)]
        eye_b = (row_b == (lane_b & (b - 1))).astype(F32)
        return blocks, eye_b

    def lane_blockdiag(x, blocks):
        z = jnp.zeros_like(x)
        return jnp.concatenate([jnp.where(m, x, z) for m in blocks], axis=0)

    def bd_parts(x, blocks):
        hi, lo = _split(x)
        return lane_blockdiag(hi, blocks), lane_blockdiag(lo, blocks)

    def diag_blocks(a, b):
        lane_2b = lax.broadcasted_iota(jnp.int32, (b, pw), 1)
        upper = (lane_2b & (2 * b - 1)) < b
        return jnp.where(upper, a[:b], a[b:])

    packed = {c: {u: pre[u]["neg_a"] for u in units}}
    size = c
    while size > base:
        packed[size // 2] = {u: diag_blocks(packed[size][u], size // 2) for u in units}
        size //= 2

    blocks, eye_b = masks(base)
    nil = dict(packed[base])
    t_blk = {u: eye_b + nil[u] for u in units}
    for u in units:
        nil[u], = _dot3_rows([nil[u]], *bd_parts(nil[u], blocks))
    levels = base.bit_length() - 2
    for lvl in range(levels):
        rhs = {u: bd_parts(nil[u], blocks) for u in units}
        for u in units:
            if lvl + 1 < levels:
                t_inc, nil[u] = _dot3_rows([t_blk[u], nil[u]], *rhs[u])
            else:
                t_inc, = _dot3_rows([t_blk[u]], *rhs[u])
            t_blk[u] = t_blk[u] + t_inc

    size = base
    while size < c:
        blocks, _ = masks(size)
        lane_s = lax.broadcasted_iota(jnp.int32, (size, pw), 1)
        left = (lane_s & (2 * size - 1)) < size
        x21 = {}
        for u in units:
            n21 = jnp.where(left, packed[2 * size][u][size:], 0.0)
            x21[u], = _dot3_rows([n21], *bd_parts(t_blk[u], blocks))
        grown = {}
        for u in units:
            x_hi, x_lo = _split(x21[u])
            z = jnp.zeros_like(x_hi)
            rhs_hi = jnp.concatenate([jnp.where(blocks[r - 1], x_hi, z) if r % 2 else z
                                      for r in range(pw // size)], axis=0)
            rhs_lo = jnp.concatenate([jnp.where(blocks[r - 1], x_lo, z) if r % 2 else z
                                      for r in range(pw // size)], axis=0)
            t21, = _dot3_rows([t_blk[u]], rhs_hi, rhs_lo)
            grown[u] = jnp.concatenate([jnp.where(left, t_blk[u], 0.0),
                                        jnp.where(left, t21, t_blk[u])], axis=0)
        t_blk = grown
        size *= 2
    t_inv = t_blk
    u_mat, w_mat = {}, {}
    for u in units:
        t_b = t_inv[u].astype(BF16)
        u_mat[u] = jnp.dot(t_b, pre[u]["vb_bd"], preferred_element_type=F32)
        w_mat[u] = jnp.dot(t_b, pre[u]["kbe_bd"], preferred_element_type=F32).astype(BF16)

    states = [state_ref[p] for p in range(n_pairs)]
    for ci in range(n_chunks):
        rows = slice(ci * c, (ci + 1) * c)
        for p in range(n_pairs):
            u = (ci, p)
            cols = slice(p * pw, (p + 1) * pw)
            state = states[p]
            ws_qs = jnp.dot(jnp.concatenate([w_mat[u], pre[u]["q_exp"]], axis=0), state.astype(BF16),
                            preferred_element_type=F32)
            v_new_b = (u_mat[u] - ws_qs[:c]).astype(BF16)
            out = ws_qs[c:] + jnp.dot(pre[u]["qk"], blockdiag(v_new_b), preferred_element_type=F32)
            kv = lax.dot_general(pre[u]["k_dec"], v_new_b, TN_DIMS, preferred_element_type=F32)
            states[p] = jnp.where(same_head, state * pre[u]["exp_last"] + kv, 0.0)
            sq = out * out
            ms = jnp.concatenate(
                [jnp.broadcast_to(jnp.mean(sq[:, :hd], axis=-1, keepdims=True), (c, hd)),
                 jnp.broadcast_to(jnp.mean(sq[:, hd:], axis=-1, keepdims=True), (c, hd))], axis=1)
            out = out * lax.rsqrt(ms + NORM_EPS) * norm_g * sz_ref[rows, cols].astype(F32)
            o_ref[rows, cols] = out.astype(BF16)
    for p in range(n_pairs):
        state_ref[p] = states[p]


def _delta(qkv, sz, gb, norm_g, *, bsz, seq, tc):
    dn_dim = sz.shape[1]
    nl = seq // tc
    tok = lambda b, i: (b * nl + i, 0)
    return pl.pallas_call(
        functools.partial(_delta_kernel, tc=tc),
        grid=(bsz, nl),
        in_specs=[pl.BlockSpec((tc, dn_dim), lambda b, i: (b * nl + i, 0)),
                  pl.BlockSpec((tc, dn_dim), lambda b, i: (b * nl + i, 1)),
                  pl.BlockSpec((tc, dn_dim), lambda b, i: (b * nl + i, 2)),
                  pl.BlockSpec((tc, LANES), tok),
                  pl.BlockSpec((tc, dn_dim), tok),
                  _const_spec(norm_g.shape)],
        out_specs=pl.BlockSpec((tc, dn_dim), tok),
        out_shape=jax.ShapeDtypeStruct((bsz * seq, dn_dim), BF16),
        scratch_shapes=[pltpu.VMEM((DN_HEADS // 2, 2 * DN_HEAD_DIM, 2 * DN_HEAD_DIM), F32)],
        compiler_params=_params(2),
        name="delta",
    )(qkv, qkv, qkv, gb, sz, norm_g)


def _sgu_body(x_ref, wuv_ref, wgate_ref, lng_ref, lnb_ref, sw_ref, sb_ref, wso_ref, u_ref, v_ref, gated_ref,
              *, tm, sgu_dim, d_model):
    xb = x_ref[...].astype(BF16)
    inv_sqrt2 = 2.0 ** -0.5
    for n in range(0, 2 * sgu_dim, 256):
        uv = jnp.dot(xb, wuv_ref[:, n:n + 256], preferred_element_type=F32)
        act = 0.5 * uv * (1.0 + lax.erf(uv * inv_sqrt2))
        if n < sgu_dim:
            u_ref[:, n:n + 256] = act
        else:
            v_ref[:, n - sgu_dim:n - sgu_dim + 256] = act

    v = v_ref[...]
    vc = v - jnp.mean(v, axis=-1, keepdims=True)
    var = jnp.mean(vc * vc, axis=-1, keepdims=True)
    v_ref[...] = vc * lax.rsqrt(var + LN_EPS) * lng_ref[...] + lnb_ref[...]

    row = lax.broadcasted_iota(jnp.int32, (SGU_CHUNK, SGU_CHUNK), 0)
    col = lax.broadcasted_iota(jnp.int32, (SGU_CHUNK, SGU_CHUNK), 1)
    causal = row >= col
    for g in range(SGU_GROUPS):
        cols = slice(g * LANES, (g + 1) * LANES)
        w_causal = jnp.where(causal, sw_ref[g], 0.0).astype(BF16)
        bias = sb_ref[:, g:g + 1]
        for ci in range(tm // SGU_CHUNK):
            rows = slice(ci * SGU_CHUNK, (ci + 1) * SGU_CHUNK)
            mixed = jnp.dot(w_causal, v_ref[rows, cols].astype(BF16), preferred_element_type=F32) + bias
            gated_ref[rows, cols] = (u_ref[rows, cols] * mixed).astype(BF16)

    gated = gated_ref[...]
    ys, sgd = [], []
    for n in range(0, d_model, 256):
        y_sgu = jnp.dot(gated, wso_ref[:, n:n + 256], preferred_element_type=F32)
        gate_sgu = jnp.dot(xb, wgate_ref[:, d_model + n:d_model + n + 256], preferred_element_type=F32)
        ys.append(_sigmoid(gate_sgu) * y_sgu)
        gate_dn = jnp.dot(xb, wgate_ref[:, n:n + 256], preferred_element_type=F32)
        sgd.append(_sigmoid(gate_dn))
    return jnp.concatenate(ys, axis=1), jnp.concatenate(sgd, axis=1)


def _layer_norm(r, g, b):
    rc = r - jnp.mean(r, axis=-1, keepdims=True)
    var = jnp.mean(rc * rc, axis=-1, keepdims=True)
    return rc * lax.rsqrt(var + LN_EPS) * g + b


def _merge_kernel(x_ref, o_ref, wuv_ref, wgate_ref, lng_ref, lnb_ref, sw_ref, sb_ref, wso_ref,
                  wdn_ref, wout_ref, g_ref, b_ref, wr_ref, br_ref,
                  h_ref, rinfo_ref, dest_ref, nsub_ref, info_ref, u_ref, v_ref, gated_ref,
                  *, alpha, tm, sgu_dim, d_model):
    ys, sgd = _sgu_body(x_ref, wuv_ref, wgate_ref, lng_ref, lnb_ref, sw_ref, sb_ref, wso_ref,
                        u_ref, v_ref, gated_ref, tm=tm, sgu_dim=sgu_dim, d_model=d_model)
    y_dn = jnp.dot(o_ref[...], wdn_ref[...], preferred_element_type=F32)
    y = sgd * y_dn + ys
    mix = jnp.dot(y.astype(BF16), wout_ref[...], preferred_element_type=F32)
    h = _layer_norm(alpha * x_ref[...] + mix, g_ref[...], b_ref[...])
    h_ref[...] = h

    h_hi, h_lo = _split(h)
    parts = jnp.dot(jnp.concatenate([h_hi, h_lo], axis=0), wr_ref[...], preferred_element_type=F32)
    logits = (parts[:tm, :LANES] + parts[tm:, :LANES]) + (parts[:tm, LANES:] + parts[tm:, LANES:]) + br_ref[...]
    lt = logits.T
    row_of = lambda i: lt[i:i + 1, :]
    g_logit = [row_of(g) for g in range(N_GROUPS)]
    g_max = functools.reduce(jnp.maximum, g_logit)
    g_idx = jnp.full(g_max.shape, N_GROUPS - 1, jnp.int32)
    for g in range(N_GROUPS - 2, -1, -1):
        g_idx = jnp.where(g_logit[g] == g_max, g, g_idx)
    group_p = 1.0 / functools.reduce(jnp.add, [jnp.exp(gl - g_max) for gl in g_logit])
    e_logit = []
    for j in range(EXPERTS_PER_GROUP):
        sel = row_of(N_GROUPS + (N_GROUPS - 1) * EXPERTS_PER_GROUP + j)
        for g in range(N_GROUPS - 2, -1, -1):
            sel = jnp.where(g_idx == g, row_of(N_GROUPS + g * EXPERTS_PER_GROUP + j), sel)
        e_logit.append(sel)
    e1 = functools.reduce(jnp.maximum, e_logit)
    j1 = jnp.full(e1.shape, EXPERTS_PER_GROUP - 1, jnp.int32)
    for j in range(EXPERTS_PER_GROUP - 2, -1, -1):
        j1 = jnp.where(e_logit[j] == e1, j, j1)
    rest = [jnp.where(j1 == j, -jnp.inf, e_logit[j]) for j in range(EXPERTS_PER_GROUP)]
    e2 = functools.reduce(jnp.maximum, rest)
    j2 = jnp.full(e2.shape, EXPERTS_PER_GROUP - 1, jnp.int32)
    for j in range(EXPERTS_PER_GROUP - 2, -1, -1):
        j2 = jnp.where(rest[j] == e2, j, j2)
    r = jnp.exp(e2 - e1)
    p1 = group_p / (1.0 + r)
    p2 = group_p * r / (1.0 + r)
    p1_hi = p1.astype(BF16).astype(F32)
    p2_hi = p2.astype(BF16).astype(F32)

    onehot = [jnp.where(g_idx == g, 1.0, 0.0) for g in range(N_GROUPS)]
    trow = lax.broadcasted_iota(jnp.int32, (tm, tm), 0)
    tcol = lax.broadcasted_iota(jnp.int32, (tm, tm), 1)
    earlier = jnp.where(trow < tcol, 1.0, 0.0).astype(BF16)
    onehot_rows = jnp.concatenate(onehot + [jnp.zeros((8 - N_GROUPS, tm), F32)], axis=0).astype(BF16)
    rank = jnp.dot(onehot_rows, earlier, preferred_element_type=F32)
    lane1 = lax.broadcasted_iota(jnp.int32, (1, LANES), 1)
    nsub_row = jnp.zeros((1, LANES), F32)
    dest_row = jnp.zeros((1, tm), F32)
    start = jnp.zeros((1, 1), F32)
    for g in range(N_GROUPS):
        count = jnp.sum(onehot[g], axis=-1, keepdims=True)
        n_sub = jnp.floor((count + (SUB_ROWS - 1)) * (1.0 / SUB_ROWS))
        dest_row = dest_row + onehot[g] * (start * SUB_ROWS + rank[g:g + 1, :])
        nsub_row = jnp.where(lane1 == g, n_sub, jnp.where(lane1 == N_GROUPS + g, start, nsub_row))
        start = start + n_sub
    dest_ref[0] = dest_row.astype(jnp.int32)
    nsub_ref[0] = nsub_row.astype(jnp.int32)
    info_ref[...] = jnp.zeros(info_ref.shape, F32)
    for j in range(EXPERTS_PER_GROUP):
        info_ref[j:j + 1, :] = jnp.where(j1 == j, p1_hi, jnp.where(j2 == j, p2_hi, 0.0))
        info_ref[EXPERTS_PER_GROUP + j:EXPERTS_PER_GROUP + j + 1, :] = jnp.where(
            j1 == j, p1 - p1_hi, jnp.where(j2 == j, p2 - p2_hi, 0.0))
    info_ref[2 * EXPERTS_PER_GROUP:2 * EXPERTS_PER_GROUP + 1, :] = dest_row
    rinfo_ref[...] = info_ref[...].T


def _merge(x2, o, wuv, wgate, sgu_ln_g, sgu_ln_b, spatial_w, spatial_b_t, wso, wdn, wout, ln_g, ln_b, wr, br,
           *, tm, alpha):
    t, d = x2.shape
    sgu_dim = wso.shape[0]
    nb = t // tm
    tok = lambda i: (i, 0)
    consts = (wuv, wgate, sgu_ln_g, sgu_ln_b, spatial_w, spatial_b_t, wso, wdn, wout, ln_g, ln_b, wr, br)
    return pl.pallas_call(
        functools.partial(_merge_kernel, alpha=alpha, tm=tm, sgu_dim=sgu_dim, d_model=d),
        grid=(nb,),
        in_specs=[pl.BlockSpec((tm, d), tok), pl.BlockSpec((tm, o.shape[1]), tok)]
        + [_const_spec(c.shape) for c in consts],
        out_specs=[pl.BlockSpec((tm, d), tok), pl.BlockSpec((tm, LANES), tok),
                   pl.BlockSpec((1, 1, tm), lambda i: (i, 0, 0)),
                   pl.BlockSpec((1, 1, LANES), lambda i: (i, 0, 0))],
        out_shape=[jax.ShapeDtypeStruct((t, d), F32), jax.ShapeDtypeStruct((t, LANES), F32),
                   jax.ShapeDtypeStruct((nb, 1, tm), jnp.int32),
                   jax.ShapeDtypeStruct((nb, 1, LANES), jnp.int32)],
        scratch_shapes=[pltpu.VMEM((LANES, tm), F32),
                        pltpu.VMEM((tm, sgu_dim), F32), pltpu.VMEM((tm, sgu_dim), F32),
                        pltpu.VMEM((tm, sgu_dim), BF16)],
        compiler_params=_params(1),
        name="sgu_merge",
    )(x2, o, *consts)


def _moe_kernel(nsub_ref, h_ref, rinfo_ref, dest_ref, wg_ref, wu_ref, wd_ref, g_ref, b_ref, out_ref, acc_ref,
                *, alpha, tm):
    blk = pl.program_id(0)
    h = h_ref[...]
    hb = h.astype(BF16)
    rinfo = rinfo_ref[...]
    lane = lax.broadcasted_iota(jnp.int32, rinfo.shape, 1)
    cw_parts = jnp.where(lane < 2 * EXPERTS_PER_GROUP, rinfo, 0.0).astype(BF16)
    dest_col = rinfo[:, 2 * EXPERTS_PER_GROUP:2 * EXPERTS_PER_GROUP + 1].astype(jnp.int32)
    dest_row = dest_ref[0]
    sub_row = lax.broadcasted_iota(jnp.int32, (SUB_ROWS, tm), 0)
    sub_lane = lax.broadcasted_iota(jnp.int32, (tm, SUB_ROWS), 1)

    def sub_tiles(groups, first_rows):
        n = len(groups)
        gather = [jnp.where(sub_row + r == dest_row, 1.0, 0.0).astype(BF16) for r in first_rows]
        scatter = [jnp.where(sub_lane + r == dest_col, 1.0, 0.0).astype(BF16) for r in first_rows]
        xs = [jnp.dot(gather[i], hb, preferred_element_type=F32).astype(BF16) for i in range(n)]
        cws = [jnp.dot(gather[i], cw_parts, preferred_element_type=F32) for i in range(n)]
        y = [None] * n
        for j in range(EXPERTS_PER_GROUP):
            hg = [jnp.dot(xs[i], wg_ref[groups[i] * EXPERTS_PER_GROUP + j], preferred_element_type=F32)
                  for i in range(n)]
            hu = [jnp.dot(xs[i], wu_ref[groups[i] * EXPERTS_PER_GROUP + j], preferred_element_type=F32)
                  for i in range(n)]
            for i in range(n):
                c_e = cws[i][:, j:j + 1] + cws[i][:, EXPERTS_PER_GROUP + j:EXPERTS_PER_GROUP + j + 1]
                hm = (_silu(hg[i]) * hu[i] * c_e).astype(BF16)
                part = jnp.dot(hm, wd_ref[groups[i] * EXPERTS_PER_GROUP + j], preferred_element_type=F32)
                y[i] = part if y[i] is None else y[i] + part
        out = [jnp.dot(scatter[i], y[i].astype(BF16), preferred_element_type=F32) for i in range(n)]
        return functools.reduce(jnp.add, out)

    n_sub = [nsub_ref[blk * LANES + g] for g in range(N_GROUPS)]
    first = [nsub_ref[blk * LANES + N_GROUPS + g] for g in range(N_GROUPS)]
    past_end = (tm // SUB_ROWS + N_GROUPS) * SUB_ROWS
    acc_ref[...] = sub_tiles(list(range(N_GROUPS)),
                             [jnp.where(n_sub[g] > 0, first[g] * SUB_ROWS, past_end) for g in range(N_GROUPS)])

    for g in range(N_GROUPS):
        def later_sub_tile(s, carry, g=g):
            acc_ref[...] += sub_tiles([g], [(first[g] + s) * SUB_ROWS])
            return carry

        lax.fori_loop(1, n_sub[g], later_sub_tile, 0)

    out_ref[...] = _layer_norm(alpha * h + acc_ref[...], g_ref[...], b_ref[...])


def _moe(h1, rinfo, dest, nsub, wg, wu, wd, ln_g, ln_b, *, tm, alpha):
    t, d = h1.shape
    nd = lambda shape: pl.BlockSpec(shape, lambda i, ns: (0,) * len(shape), pipeline_mode=pl.Buffered(1))
    tok = lambda i, ns: (i, 0)
    grid_spec = pltpu.PrefetchScalarGridSpec(
        num_scalar_prefetch=1,
        grid=(t // tm,),
        in_specs=[pl.BlockSpec((tm, d), tok), pl.BlockSpec((tm, LANES), tok),
                  pl.BlockSpec((1, 1, tm), lambda i, ns: (i, 0, 0)),
                  nd(wg.shape), nd(wu.shape), nd(wd.shape), nd(ln_g.shape), nd(ln_b.shape)],
        out_specs=pl.BlockSpec((tm, d), tok),
        scratch_shapes=[pltpu.VMEM((tm, d), F32)])
    return pl.pallas_call(
        functools.partial(_moe_kernel, alpha=alpha, tm=tm),
        grid_spec=grid_spec,
        out_shape=jax.ShapeDtypeStruct((t, d), F32),
        compiler_params=_params(1),
        name="moe",
    )(nsub.reshape(-1), h1, rinfo, dest, wg, wu, wd, ln_g, ln_b)


def _pad_lanes(a, width=LANES):
    return jnp.pad(a, [(0, 0)] * (a.ndim - 1) + [(0, width - a.shape[-1])])


def _layer(h, w_in, conv_w, a_log, dt_bias, dn_norm_g, w_dn_out, sgu_ln_g, sgu_ln_b,
           spatial_w, spatial_b, w_sgu_out, w_out, ln1_g, ln1_b,
           router_group_w, router_group_b, router_expert_w, router_expert_b,
           expert_w_gate, expert_w_up, expert_w_down, ln2_g, ln2_b, *, alpha):
    bsz, seq, d = h.shape
    dn_dim = w_dn_out.shape[0]
    sgu_dim = w_sgu_out.shape[0]
    x2 = h.reshape(bsz * seq, d)

    o_z = 3 * dn_dim
    o_a = o_z + dn_dim
    o_uv = o_a + 2 * DN_HEADS
    o_gate = o_uv + 2 * sgu_dim
    wqkv = w_in[:, :o_z].astype(BF16)
    wz = w_in[:, o_z:o_a].astype(BF16)
    wab = _pad_lanes(w_in[:, o_a:o_uv]).astype(BF16)
    wuv = w_in[:, o_uv:o_gate].astype(BF16)
    wgate = w_in[:, o_gate:].astype(BF16)
    gpar = jnp.stack([_pad_lanes(-jnp.exp(a_log.astype(F32))), _pad_lanes(dt_bias.astype(F32))])

    qkv, sz, gb = _proj_dn(x2, wqkv, wz, wab, conv_w, gpar, bsz=bsz, seq=seq, tm=PROJ_TILE)
    o = _delta(qkv, sz, gb, dn_norm_g.reshape(1, -1), bsz=bsz, seq=seq, tc=DELTA_TILE)
    wr = _pad_lanes(jnp.concatenate([router_group_w, router_expert_w], axis=1))
    wr = jnp.concatenate(_split(wr), axis=1)
    br = _pad_lanes(jnp.concatenate([router_group_b, router_expert_b]).reshape(1, -1))
    h1, rinfo, dest, nsub = _merge(x2, o, wuv, wgate, sgu_ln_g.reshape(1, -1), sgu_ln_b.reshape(1, -1),
                                   spatial_w, spatial_b.T, w_sgu_out.astype(BF16),
                                   w_dn_out.astype(BF16), w_out.astype(BF16),
                                   ln1_g.reshape(1, -1), ln1_b.reshape(1, -1), wr, br,
                                   tm=MOE_BLOCK, alpha=alpha)
    h2 = _moe(h1, rinfo, dest, nsub, expert_w_gate.astype(BF16), expert_w_up.astype(BF16),
              expert_w_down.astype(BF16), ln2_g.reshape(1, -1), ln2_b.reshape(1, -1),
              tm=MOE_BLOCK, alpha=alpha)
    return h2.reshape(bsz, seq, d)


def kernel(x, w_in, conv_w, a_log, dt_bias, dn_norm_g, w_dn_out, sgu_ln_g, sgu_ln_b, spatial_w, spatial_b, w_sgu_out, w_out, ln1_g, ln1_b, router_group_w, router_group_b, router_expert_w, router_expert_b, expert_w_gate, expert_w_up, expert_w_down, ln2_g, ln2_b):
    depth = w_in.shape[0]
    alpha = (2.0 * depth) ** 0.25
    h = x
    for l in range(depth):
        h = _layer(h, w_in[l], conv_w[l], a_log[l], dt_bias[l], dn_norm_g[l], w_dn_out[l],
                   sgu_ln_g[l], sgu_ln_b[l], spatial_w[l], spatial_b[l], w_sgu_out[l], w_out[l],
                   ln1_g[l], ln1_b[l], router_group_w[l], router_group_b[l],
                   router_expert_w[l], router_expert_b[l],
                   expert_w_gate[l], expert_w_up[l], expert_w_down[l], ln2_g[l], ln2_b[l], alpha=alpha)
    return h
```

```python
import functools

import jax
import jax.numpy as jnp
from jax import lax
from jax.experimental import pallas as pl
from jax.experimental.pallas import tpu as pltpu

F32 = jnp.float32
BF16 = jnp.bfloat16

LANES = 128
DN_HEADS = 8
DN_HEAD_DIM = 128
CONV_WIDTH = 4
SGU_GROUPS = 8
SGU_CHUNK = 128
N_GROUPS = 4
EXPERTS_PER_GROUP = 4
LN_EPS = 1e-5
NORM_EPS = 1e-6
DELTA_CHUNK = 128
INVERSE_BASE_BLOCK = 32
CONV_PAD = 8
PROJ_TILE = 1024
DELTA_TILE = 512
MOE_BLOCK = 512
SUB_ROWS = 160
VMEM_LIMIT = 56 * 1024 * 1024

NT_DIMS = (((1,), (1,)), ((), ()))
TN_DIMS = (((0,), (0,)), ((), ()))


def _dot_f32(a, b):
    return jnp.dot(a, b, preferred_element_type=F32, precision=lax.Precision.HIGHEST)


def _split(x):
    hi = x.astype(BF16)
    return hi, (x - hi.astype(F32)).astype(BF16)


def _dot3_rows(lhs, b_hi, b_lo):
    m = lhs[0].shape[0]
    n = len(lhs)
    parts = [_split(a) for a in lhs]
    his = [hi for hi, _ in parts]
    los = [lo for _, lo in parts]
    r_hi = jnp.dot(jnp.concatenate(his + los, axis=0), b_hi, preferred_element_type=F32)
    r_lo = jnp.dot(jnp.concatenate(his, axis=0) if n > 1 else his[0], b_lo, preferred_element_type=F32)
    return [r_hi[i * m:(i + 1) * m] + r_hi[(n + i) * m:(n + i + 1) * m] + r_lo[i * m:(i + 1) * m]
            for i in range(n)]


def _sigmoid(x):
    return 0.5 + 0.5 * jnp.tanh(0.5 * x)


def _silu(x):
    half = 0.5 * x
    return half + half * jnp.tanh(half)


def _const_spec(shape):
    nd = len(shape)
    return pl.BlockSpec(shape, lambda *_: (0,) * nd, pipeline_mode=pl.Buffered(1))


def _params(n_axes):
    return pltpu.CompilerParams(dimension_semantics=("arbitrary",) * n_axes,
                                vmem_limit_bytes=VMEM_LIMIT)


def _proj_dn_kernel(x_ref, wqkv_ref, wz_ref, wab_ref, conv_ref, gpar_ref,
                    qkv_ref, sz_ref, gb_ref, ext_ref, *, tm, dn_dim):
    i = pl.program_id(1)
    xb = x_ref[...].astype(BF16)
    n_qkv = 3 * dn_dim

    @pl.when(i == 0)
    def _():
        ext_ref[0:CONV_PAD, :] = jnp.zeros((CONV_PAD, n_qkv), F32)

    for n in range(0, n_qkv, 256):
        ext_ref[CONV_PAD:CONV_PAD + tm, n:n + 256] = jnp.dot(
            xb, wqkv_ref[:, n:n + 256], preferred_element_type=F32)

    for n in range(0, n_qkv, LANES):
        w = conv_ref[:, n:n + LANES]
        y = w[3:4, :] * ext_ref[CONV_PAD:CONV_PAD + tm, n:n + LANES]
        for j in range(1, CONV_WIDTH):
            y = y + w[3 - j:4 - j, :] * ext_ref[CONV_PAD - j:CONV_PAD - j + tm, n:n + LANES]
        y = _silu(y)
        if n < 2 * dn_dim:
            y = y * lax.rsqrt(jnp.sum(y * y, axis=-1, keepdims=True) + NORM_EPS)
            if n < dn_dim:
                y = y * (DN_HEAD_DIM ** -0.5)
        qkv_ref[:, n:n + LANES] = y.astype(BF16)

    ext_ref[0:CONV_PAD, :] = ext_ref[tm:tm + CONV_PAD, :]

    for n in range(0, dn_dim, 256):
        z = jnp.dot(xb, wz_ref[:, n:n + 256], preferred_element_type=F32)
        sz_ref[:, n:n + 256] = _silu(z).astype(BF16)

    ab = jnp.dot(xb, wab_ref[...], preferred_element_type=F32)
    neg_exp_alog = gpar_ref[0:1, :]
    dt_bias = gpar_ref[1:2, :]
    sp_in = ab + dt_bias
    softplus = jnp.maximum(sp_in, 0.0) + jnp.log1p(jnp.exp(-jnp.abs(sp_in)))
    lane = lax.broadcasted_iota(jnp.int32, ab.shape, 1)
    gb_ref[...] = jnp.where(lane < DN_HEADS, neg_exp_alog * softplus, _sigmoid(ab))


def _proj_dn(x2, wqkv, wz, wab, conv_w, gpar, *, bsz, seq, tm):
    d = x2.shape[1]
    dn_dim = wz.shape[1]
    nl = seq // tm
    tok = lambda b, i: (b * nl + i, 0)
    return pl.pallas_call(
        functools.partial(_proj_dn_kernel, tm=tm, dn_dim=dn_dim),
        grid=(bsz, nl),
        in_specs=[pl.BlockSpec((tm, d), tok),
                  _const_spec(wqkv.shape), _const_spec(wz.shape), _const_spec(wab.shape),
                  _const_spec(conv_w.shape), _const_spec(gpar.shape)],
        out_specs=[pl.BlockSpec((tm, 3 * dn_dim), tok),
                   pl.BlockSpec((tm, dn_dim), tok),
                   pl.BlockSpec((tm, LANES), tok)],
        out_shape=[jax.ShapeDtypeStruct((bsz * seq, 3 * dn_dim), BF16),
                   jax.ShapeDtypeStruct((bsz * seq, dn_dim), BF16),
                   jax.ShapeDtypeStruct((bsz * seq, LANES), F32)],
        scratch_shapes=[pltpu.VMEM((tm + CONV_PAD, 3 * dn_dim), F32)],
        compiler_params=_params(2),
        name="proj_dn",
    )(x2, wqkv, wz, wab, conv_w, gpar)


def _delta_kernel(q_ref, k_ref, v_ref, gb_ref, sz_ref, ng_ref, wg32_ref, wu32_ref, wd32_ref,
                  o_ref, wg_ref, wu_ref, wd_ref, state_ref, *, tc):
    c = DELTA_CHUNK
    hd = DN_HEAD_DIM
    pw = 2 * hd
    assert c == hd
    wg_ref[...] = wg32_ref[...].astype(BF16)
    wu_ref[...] = wu32_ref[...].astype(BF16)
    wd_ref[...] = wd32_ref[...].astype(BF16)

    @pl.when(pl.program_id(1) == 0)
    def _():
        state_ref[...] = jnp.zeros(state_ref.shape, F32)

    row = lax.broadcasted_iota(jnp.int32, (c, pw), 0)
    lane = lax.broadcasted_iota(jnp.int32, (c, pw), 1)
    first = lane < hd
    col = jnp.where(first, lane, lane - hd)
    causal = row >= col
    strict = row > col
    row_s = lax.broadcasted_iota(jnp.int32, (pw, pw), 0)
    lane_s = lax.broadcasted_iota(jnp.int32, (pw, pw), 1)
    same_head = (row_s < hd) == (lane_s < hd)
    trow = lax.broadcasted_iota(jnp.int32, (c, c), 0)
    tcol = lax.broadcasted_iota(jnp.int32, (c, c), 1)
    tri = (trow >= tcol).astype(F32)
    norm_g = jnp.concatenate([ng_ref[...], ng_ref[...]], axis=1)

    def blockdiag(x):
        z = jnp.zeros_like(x)
        return jnp.concatenate([jnp.where(first, x, z), jnp.where(first, z, x)], axis=0)

    def pair_cols(a, j0, j1):
        return jnp.concatenate([jnp.broadcast_to(a[:, j0:j0 + 1], (c, hd)),
                                jnp.broadcast_to(a[:, j1:j1 + 1], (c, hd))], axis=1)

    def pair_rows(a, j0, j1, r):
        return jnp.concatenate([jnp.broadcast_to(a[r:r + 1, j0:j0 + 1], (1, hd)),
                                jnp.broadcast_to(a[r:r + 1, j1:j1 + 1], (1, hd))], axis=1)

    n_pairs = DN_HEADS // 2
    n_chunks = tc // c
    units = [(ci, p) for ci in range(n_chunks) for p in range(n_pairs)]

    gcums = []
    for ci in range(n_chunks):
        gb = gb_ref[ci * c:(ci + 1) * c, :]
        gcum = _dot_f32(tri, gb)
        gcums.append((gb, gcum, gcum.T))
    pre = {}
    for ci, p in units:
        gb, gcum, gcum_t = gcums[ci]
        h0, h1 = 2 * p, 2 * p + 1
        rows = slice(ci * c, (ci + 1) * c)
        cols = slice(p * pw, (p + 1) * pw)
        q = q_ref[rows, cols].astype(F32)
        k_b = k_ref[rows, cols]
        k = k_b.astype(F32)
        v = v_ref[rows, cols].astype(F32)
        g_col = pair_cols(gcum, h0, h1)
        g_row = jnp.concatenate([gcum_t[h0:h0 + 1, :], gcum_t[h1:h1 + 1, :]], axis=1)
        g_last = pair_rows(gcum, h0, h1, c - 1)
        beta = pair_cols(gb, DN_HEADS + h0, DN_HEADS + h1)
        decay = jnp.where(causal, jnp.exp(jnp.where(causal, g_col - g_row, 0.0)), 0.0)
        exp_g = jnp.exp(g_col)
        k_beta = k * beta
        aq = lax.dot_general(jnp.concatenate([k_beta, q], axis=0).astype(BF16), blockdiag(k_b),
                             NT_DIMS, preferred_element_type=F32)
        pre[ci, p] = dict(
            neg_a=jnp.where(strict, -(aq[:c] * decay), 0.0),
            qk=(aq[c:] * decay).astype(BF16),
            vb_bd=blockdiag((v * beta).astype(BF16)),
            kbe_bd=blockdiag((k_beta * exp_g).astype(BF16)),
            q_exp=(q * exp_g).astype(BF16),
            k_dec=(k * jnp.exp(g_last - g_col)).astype(BF16),
            exp_last=jnp.exp(g_last))

    base = INVERSE_BASE_BLOCK

    def masks(b):
        row_b = lax.broadcasted_iota(jnp.int32, (b, pw), 0)
        lane_b = lax.broadcasted_iota(jnp.int32, (b, pw), 1)
        blk_id = lax.shift_right_logical(lane_b, b.bit_length() - 1)
        blocks = [blk_id == r for r in range(pw // b)]
        eye_b = (row_b == (lane_b & (b - 1))).astype(F32)
        return blocks, eye_b

    def lane_blockdiag(x, blocks):
        z = jnp.zeros_like(x)
        return jnp.concatenate([jnp.where(m, x, z) for m in blocks], axis=0)

    def bd_parts(x, blocks):
        hi, lo = _split(x)
        return lane_blockdiag(hi, blocks), lane_blockdiag(lo, blocks)

    def diag_blocks(a, b):
        lane_2b = lax.broadcasted_iota(jnp.int32, (b, pw), 1)
        upper = (lane_2b & (2 * b - 1)) < b
        return jnp.where(upper, a[:b], a[b:])

    packed = {c: {u: pre[u]["neg_a"] for u in units}}
    size = c
    while size > base:
        packed[size // 2] = {u: diag_blocks(packed[size][u], size // 2) for u in units}
        size //= 2

    blocks, eye_b = masks(base)
    nil = dict(packed[base])
    t_blk = {u: eye_b + nil[u] for u in units}
    for u in units:
        nil[u], = _dot3_rows([nil[u]], *bd_parts(nil[u], blocks))
    levels = base.bit_length() - 2
    for lvl in range(levels):
        rhs = {u: bd_parts(nil[u], blocks) for u in units}
        for u in units:
            if lvl + 1 < levels:
                t_inc, nil[u] = _dot3_rows([t_blk[u], nil[u]], *rhs[u])
            else:
                t_inc, = _dot3_rows([t_blk[u]], *rhs[u])
            t_blk[u] = t_blk[u] + t_inc

    size = base
    while size < c:
        blocks, _ = masks(size)
        lane_s = lax.broadcasted_iota(jnp.int32, (size, pw), 1)
        left = (lane_s & (2 * size - 1)) < size
        x21 = {}
        for u in units:
            n21 = jnp.where(left, packed[2 * size][u][size:], 0.0)
            x21[u], = _dot3_rows([n21], *bd_parts(t_blk[u], blocks))
        grown = {}
        for u in units:
            x_hi, x_lo = _split(x21[u])
            z = jnp.zeros_like(x_hi)
            rhs_hi = jnp.concatenate([jnp.where(blocks[r - 1], x_hi, z) if r % 2 else z
                                      for r in range(pw // size)], axis=0)
            rhs_lo = jnp.concatenate([jnp.where(blocks[r - 1], x_lo, z) if r % 2 else z
                                      for r in range(pw // size)], axis=0)
            t21, = _dot3_rows([t_blk[u]], rhs_hi, rhs_lo)
            grown[u] = jnp.concatenate([jnp.where(left, t_blk[u], 0.0),
                                        jnp.where(left, t21, t_blk[u])], axis=0)
        t_blk = grown
        size *= 2
    t_inv = t_blk
    u_mat, w_mat = {}, {}
    for u in units:
        t_b = t_inv[u].astype(BF16)
        u_mat[u] = jnp.dot(t_b, pre[u]["vb_bd"], preferred_element_type=F32)
        w_mat[u] = jnp.dot(t_b, pre[u]["kbe_bd"], preferred_element_type=F32).astype(BF16)

    states = [state_ref[p] for p in range(n_pairs)]
    for ci in range(n_chunks):
        rows = slice(ci * c, (ci + 1) * c)
        for p in range(n_pairs):
            u = (ci, p)
            cols = slice(p * pw, (p + 1) * pw)
            state = states[p]
            ws_qs = jnp.dot(jnp.concatenate([w_mat[u], pre[u]["q_exp"]], axis=0), state.astype(BF16),
                            preferred_element_type=F32)
            v_new_b = (u_mat[u] - ws_qs[:c]).astype(BF16)
            out = ws_qs[c:] + jnp.dot(pre[u]["qk"], blockdiag(v_new_b), preferred_element_type=F32)
            kv = lax.dot_general(pre[u]["k_dec"], v_new_b, TN_DIMS, preferred_element_type=F32)
            states[p] = jnp.where(same_head, state * pre[u]["exp_last"] + kv, 0.0)
            sq = out * out
            ms = jnp.concatenate(
                [jnp.broadcast_to(jnp.mean(sq[:, :hd], axis=-1, keepdims=True), (c, hd)),
                 jnp.broadcast_to(jnp.mean(sq[:, hd:], axis=-1, keepdims=True), (c, hd))], axis=1)
            out = out * lax.rsqrt(ms + NORM_EPS) * norm_g * sz_ref[rows, cols].astype(F32)
            o_ref[rows, cols] = out.astype(BF16)
    for p in range(n_pairs):
        state_ref[p] = states[p]


def _delta(qkv, sz, gb, norm_g, expert_weights, *, bsz, seq, tc):
    dn_dim = sz.shape[1]
    nl = seq // tc
    tok = lambda b, i: (b * nl + i, 0)
    n_experts = expert_weights[0].shape[0]
    slabs = (bsz * nl) // n_experts
    assert slabs * n_experts == bsz * nl and all(w.shape[1] % (8 * slabs) == 0 for w in expert_weights)
    slab_spec = lambda w: pl.BlockSpec(
        (1, w.shape[1] // slabs, w.shape[2]), lambda b, i: ((b * nl + i) // slabs, (b * nl + i) % slabs, 0))
    return pl.pallas_call(
        functools.partial(_delta_kernel, tc=tc),
        grid=(bsz, nl),
        in_specs=[pl.BlockSpec((tc, dn_dim), lambda b, i: (b * nl + i, 0)),
                  pl.BlockSpec((tc, dn_dim), lambda b, i: (b * nl + i, 1)),
                  pl.BlockSpec((tc, dn_dim), lambda b, i: (b * nl + i, 2)),
                  pl.BlockSpec((tc, LANES), tok),
                  pl.BlockSpec((tc, dn_dim), tok),
                  _const_spec(norm_g.shape)] + [slab_spec(w) for w in expert_weights],
        out_specs=[pl.BlockSpec((tc, dn_dim), tok)] + [slab_spec(w) for w in expert_weights],
        out_shape=[jax.ShapeDtypeStruct((bsz * seq, dn_dim), BF16)]
        + [jax.ShapeDtypeStruct(w.shape, BF16) for w in expert_weights],
        scratch_shapes=[pltpu.VMEM((DN_HEADS // 2, 2 * DN_HEAD_DIM, 2 * DN_HEAD_DIM), F32)],
        compiler_params=_params(2),
        name="delta",
    )(qkv, qkv, qkv, gb, sz, norm_g, *expert_weights)


def _sgu_body(x_ref, wuv_ref, wgate_ref, lng_ref, lnb_ref, sw_ref, sb_ref, wso_ref, u_ref, v_ref, gated_ref,
              *, tm, sgu_dim, d_model):
    xb = x_ref[...].astype(BF16)
    inv_sqrt2 = 2.0 ** -0.5
    for n in range(0, 2 * sgu_dim, 256):
        uv = jnp.dot(xb, wuv_ref[:, n:n + 256], preferred_element_type=F32)
        act = 0.5 * uv * (1.0 + lax.erf(uv * inv_sqrt2))
        if n < sgu_dim:
            u_ref[:, n:n + 256] = act
        else:
            v_ref[:, n - sgu_dim:n - sgu_dim + 256] = act

    v = v_ref[...]
    vc = v - jnp.mean(v, axis=-1, keepdims=True)
    var = jnp.mean(vc * vc, axis=-1, keepdims=True)
    v_ref[...] = vc * lax.rsqrt(var + LN_EPS) * lng_ref[...] + lnb_ref[...]

    row = lax.broadcasted_iota(jnp.int32, (SGU_CHUNK, SGU_CHUNK), 0)
    col = lax.broadcasted_iota(jnp.int32, (SGU_CHUNK, SGU_CHUNK), 1)
    causal = row >= col
    for g in range(SGU_GROUPS):
        cols = slice(g * LANES, (g + 1) * LANES)
        w_causal = jnp.where(causal, sw_ref[g], 0.0).astype(BF16)
        bias = sb_ref[:, g:g + 1]
        for ci in range(tm // SGU_CHUNK):
            rows = slice(ci * SGU_CHUNK, (ci + 1) * SGU_CHUNK)
            mixed = jnp.dot(w_causal, v_ref[rows, cols].astype(BF16), preferred_element_type=F32) + bias
            gated_ref[rows, cols] = (u_ref[rows, cols] * mixed).astype(BF16)

    gated = gated_ref[...]
    ys, sgd = [], []
    for n in range(0, d_model, 256):
        y_sgu = jnp.dot(gated, wso_ref[:, n:n + 256], preferred_element_type=F32)
        gate_sgu = jnp.dot(xb, wgate_ref[:, d_model + n:d_model + n + 256], preferred_element_type=F32)
        ys.append(_sigmoid(gate_sgu) * y_sgu)
        gate_dn = jnp.dot(xb, wgate_ref[:, n:n + 256], preferred_element_type=F32)
        sgd.append(_sigmoid(gate_dn))
    return jnp.concatenate(ys, axis=1), jnp.concatenate(sgd, axis=1)


def _layer_norm(r, g, b):
    rc = r - jnp.mean(r, axis=-1, keepdims=True)
    var = jnp.mean(rc * rc, axis=-1, keepdims=True)
    return rc * lax.rsqrt(var + LN_EPS) * g + b


def _merge_kernel(x_ref, o_ref, wuv_ref, wgate_ref, lng_ref, lnb_ref, sw_ref, sb_ref, wso_ref,
                  wdn_ref, wout_ref, g_ref, b_ref, wr_ref, br_ref,
                  h_ref, rinfo_ref, dest_ref, nsub_ref, info_ref, u_ref, v_ref, gated_ref,
                  *, alpha, tm, sgu_dim, d_model):
    ys, sgd = _sgu_body(x_ref, wuv_ref, wgate_ref, lng_ref, lnb_ref, sw_ref, sb_ref, wso_ref,
                        u_ref, v_ref, gated_ref, tm=tm, sgu_dim=sgu_dim, d_model=d_model)
    y_dn = jnp.dot(o_ref[...], wdn_ref[...], preferred_element_type=F32)
    y = sgd * y_dn + ys
    mix = jnp.dot(y.astype(BF16), wout_ref[...], preferred_element_type=F32)
    h = _layer_norm(alpha * x_ref[...] + mix, g_ref[...], b_ref[...])
    h_ref[...] = h

    h_hi, h_lo = _split(h)
    parts = jnp.dot(jnp.concatenate([h_hi, h_lo], axis=0), wr_ref[...], preferred_element_type=F32)
    logits = (parts[:tm, :LANES] + parts[tm:, :LANES]) + (parts[:tm, LANES:] + parts[tm:, LANES:]) + br_ref[...]
    lt = logits.T
    row_of = lambda i: lt[i:i + 1, :]
    g_logit = [row_of(g) for g in range(N_GROUPS)]
    g_max = functools.reduce(jnp.maximum, g_logit)
    g_idx = jnp.full(g_max.shape, N_GROUPS - 1, jnp.int32)
    for g in range(N_GROUPS - 2, -1, -1):
        g_idx = jnp.where(g_logit[g] == g_max, g, g_idx)
    group_p = 1.0 / functools.reduce(jnp.add, [jnp.exp(gl - g_max) for gl in g_logit])
    e_logit = []
    for j in range(EXPERTS_PER_GROUP):
        sel = row_of(N_GROUPS + (N_GROUPS - 1) * EXPERTS_PER_GROUP + j)
        for g in range(N_GROUPS - 2, -1, -1):
            sel = jnp.where(g_idx == g, row_of(N_GROUPS + g * EXPERTS_PER_GROUP + j), sel)
        e_logit.append(sel)
    e1 = functools.reduce(jnp.maximum, e_logit)
    j1 = jnp.full(e1.shape, EXPERTS_PER_GROUP - 1, jnp.int32)
    for j in range(EXPERTS_PER_GROUP - 2, -1, -1):
        j1 = jnp.where(e_logit[j] == e1, j, j1)
    rest = [jnp.where(j1 == j, -jnp.inf, e_logit[j]) for j in range(EXPERTS_PER_GROUP)]
    e2 = functools.reduce(jnp.maximum, rest)
    j2 = jnp.full(e2.shape, EXPERTS_PER_GROUP - 1, jnp.int32)
    for j in range(EXPERTS_PER_GROUP - 2, -1, -1):
        j2 = jnp.where(rest[j] == e2, j, j2)
    r = jnp.exp(e2 - e1)
    p1 = group_p / (1.0 + r)
    p2 = group_p * r / (1.0 + r)
    p1_hi = p1.astype(BF16).astype(F32)
    p2_hi = p2.astype(BF16).astype(F32)

    onehot = [jnp.where(g_idx == g, 1.0, 0.0) for g in range(N_GROUPS)]
    trow = lax.broadcasted_iota(jnp.int32, (tm, tm), 0)
    tcol = lax.broadcasted_iota(jnp.int32, (tm, tm), 1)
    earlier = jnp.where(trow < tcol, 1.0, 0.0).astype(BF16)
    onehot_rows = jnp.concatenate(onehot + [jnp.zeros((8 - N_GROUPS, tm), F32)], axis=0).astype(BF16)
    rank = jnp.dot(onehot_rows, earlier, preferred_element_type=F32)
    lane1 = lax.broadcasted_iota(jnp.int32, (1, LANES), 1)
    nsub_row = jnp.zeros((1, LANES), F32)
    dest_row = jnp.zeros((1, tm), F32)
    start = jnp.zeros((1, 1), F32)
    for g in range(N_GROUPS):
        count = jnp.sum(onehot[g], axis=-1, keepdims=True)
        n_sub = jnp.floor((count + (SUB_ROWS - 1)) * (1.0 / SUB_ROWS))
        dest_row = dest_row + onehot[g] * (start * SUB_ROWS + rank[g:g + 1, :])
        nsub_row = jnp.where(lane1 == g, n_sub, jnp.where(lane1 == N_GROUPS + g, start, nsub_row))
        start = start + n_sub
    dest_ref[0] = dest_row.astype(jnp.int32)
    nsub_ref[0] = nsub_row.astype(jnp.int32)
    info_ref[...] = jnp.zeros(info_ref.shape, F32)
    for j in range(EXPERTS_PER_GROUP):
        info_ref[j:j + 1, :] = jnp.where(j1 == j, p1_hi, jnp.where(j2 == j, p2_hi, 0.0))
        info_ref[EXPERTS_PER_GROUP + j:EXPERTS_PER_GROUP + j + 1, :] = jnp.where(
            j1 == j, p1 - p1_hi, jnp.where(j2 == j, p2 - p2_hi, 0.0))
    info_ref[2 * EXPERTS_PER_GROUP:2 * EXPERTS_PER_GROUP + 1, :] = dest_row
    rinfo_ref[...] = info_ref[...].T


def _merge(x2, o, wuv, wgate, sgu_ln_g, sgu_ln_b, spatial_w, spatial_b_t, wso, wdn, wout, ln_g, ln_b, wr, br,
           *, tm, alpha):
    t, d = x2.shape
    sgu_dim = wso.shape[0]
    nb = t // tm
    tok = lambda i: (i, 0)
    consts = (wuv, wgate, sgu_ln_g, sgu_ln_b, spatial_w, spatial_b_t, wso, wdn, wout, ln_g, ln_b, wr, br)
    return pl.pallas_call(
        functools.partial(_merge_kernel, alpha=alpha, tm=tm, sgu_dim=sgu_dim, d_model=d),
        grid=(nb,),
        in_specs=[pl.BlockSpec((tm, d), tok), pl.BlockSpec((tm, o.shape[1]), tok)]
        + [_const_spec(c.shape) for c in consts],
        out_specs=[pl.BlockSpec((tm, d), tok), pl.BlockSpec((tm, LANES), tok),
                   pl.BlockSpec((1, 1, tm), lambda i: (i, 0, 0)),
                   pl.BlockSpec((1, 1, LANES), lambda i: (i, 0, 0))],
        out_shape=[jax.ShapeDtypeStruct((t, d), F32), jax.ShapeDtypeStruct((t, LANES), F32),
                   jax.ShapeDtypeStruct((nb, 1, tm), jnp.int32),
                   jax.ShapeDtypeStruct((nb, 1, LANES), jnp.int32)],
        scratch_shapes=[pltpu.VMEM((LANES, tm), F32),
                        pltpu.VMEM((tm, sgu_dim), F32), pltpu.VMEM((tm, sgu_dim), F32),
                        pltpu.VMEM((tm, sgu_dim), BF16)],
        compiler_params=_params(1),
        name="sgu_merge",
    )(x2, o, *consts)


def _moe_kernel(nsub_ref, h_ref, rinfo_ref, dest_ref, wg_ref, wu_ref, wd_ref, g_ref, b_ref, out_ref, acc_ref,
                *, alpha, tm):
    blk = pl.program_id(0)
    h = h_ref[...]
    hb = h.astype(BF16)
    rinfo = rinfo_ref[...]
    lane = lax.broadcasted_iota(jnp.int32, rinfo.shape, 1)
    cw_parts = jnp.where(lane < 2 * EXPERTS_PER_GROUP, rinfo, 0.0).astype(BF16)
    dest_col = rinfo[:, 2 * EXPERTS_PER_GROUP:2 * EXPERTS_PER_GROUP + 1].astype(jnp.int32)
    dest_row = dest_ref[0]
    sub_row = lax.broadcasted_iota(jnp.int32, (SUB_ROWS, tm), 0)
    sub_lane = lax.broadcasted_iota(jnp.int32, (tm, SUB_ROWS), 1)

    def sub_tiles(groups, first_rows):
        n = len(groups)
        gather = [jnp.where(sub_row + r == dest_row, 1.0, 0.0).astype(BF16) for r in first_rows]
        scatter = [jnp.where(sub_lane + r == dest_col, 1.0, 0.0).astype(BF16) for r in first_rows]
        xs = [jnp.dot(gather[i], hb, preferred_element_type=F32).astype(BF16) for i in range(n)]
        cws = [jnp.dot(gather[i], cw_parts, preferred_element_type=F32) for i in range(n)]
        y = [None] * n
        for j in range(EXPERTS_PER_GROUP):
            hg = [jnp.dot(xs[i], wg_ref[groups[i] * EXPERTS_PER_GROUP + j], preferred_element_type=F32)
                  for i in range(n)]
            hu = [jnp.dot(xs[i], wu_ref[groups[i] * EXPERTS_PER_GROUP + j], preferred_element_type=F32)
                  for i in range(n)]
            for i in range(n):
                c_e = cws[i][:, j:j + 1] + cws[i][:, EXPERTS_PER_GROUP + j:EXPERTS_PER_GROUP + j + 1]
                hm = (_silu(hg[i]) * hu[i] * c_e).astype(BF16)
                part = jnp.dot(hm, wd_ref[groups[i] * EXPERTS_PER_GROUP + j], preferred_element_type=F32)
                y[i] = part if y[i] is None else y[i] + part
        out = [jnp.dot(scatter[i], y[i].astype(BF16), preferred_element_type=F32) for i in range(n)]
        return functools.reduce(jnp.add, out)

    n_sub = [nsub_ref[blk * LANES + g] for g in range(N_GROUPS)]
    first = [nsub_ref[blk * LANES + N_GROUPS + g] for g in range(N_GROUPS)]
    past_end = (tm // SUB_ROWS + N_GROUPS) * SUB_ROWS
    acc_ref[...] = sub_tiles(list(range(N_GROUPS)),
                             [jnp.where(n_sub[g] > 0, first[g] * SUB_ROWS, past_end) for g in range(N_GROUPS)])

    for g in range(N_GROUPS):
        def later_sub_tile(s, carry, g=g):
            acc_ref[...] += sub_tiles([g], [(first[g] + s) * SUB_ROWS])
            return carry

        lax.fori_loop(1, n_sub[g], later_sub_tile, 0)

    out_ref[...] = _layer_norm(alpha * h + acc_ref[...], g_ref[...], b_ref[...])


def _moe(h1, rinfo, dest, nsub, wg, wu, wd, ln_g, ln_b, *, tm, alpha):
    t, d = h1.shape
    nd = lambda shape: pl.BlockSpec(shape, lambda i, ns: (0,) * len(shape), pipeline_mode=pl.Buffered(1))
    tok = lambda i, ns: (i, 0)
    grid_spec = pltpu.PrefetchScalarGridSpec(
        num_scalar_prefetch=1,
        grid=(t // tm,),
        in_specs=[pl.BlockSpec((tm, d), tok), pl.BlockSpec((tm, LANES), tok),
                  pl.BlockSpec((1, 1, tm), lambda i, ns: (i, 0, 0)),
                  nd(wg.shape), nd(wu.shape), nd(wd.shape), nd(ln_g.shape), nd(ln_b.shape)],
        out_specs=pl.BlockSpec((tm, d), tok),
        scratch_shapes=[pltpu.VMEM((tm, d), F32)])
    return pl.pallas_call(
        functools.partial(_moe_kernel, alpha=alpha, tm=tm),
        grid_spec=grid_spec,
        out_shape=jax.ShapeDtypeStruct((t, d), F32),
        compiler_params=_params(1),
        name="moe",
    )(nsub.reshape(-1), h1, rinfo, dest, wg, wu, wd, ln_g, ln_b)


def _pad_lanes(a, width=LANES):
    return jnp.pad(a, [(0, 0)] * (a.ndim - 1) + [(0, width - a.shape[-1])])


def _layer(h, w_in, conv_w, a_log, dt_bias, dn_norm_g, w_dn_out, sgu_ln_g, sgu_ln_b,
           spatial_w, spatial_b, w_sgu_out, w_out, ln1_g, ln1_b,
           router_group_w, router_group_b, router_expert_w, router_expert_b,
           expert_w_gate, expert_w_up, expert_w_down, ln2_g, ln2_b, *, alpha):
    bsz, seq, d = h.shape
    dn_dim = w_dn_out.shape[0]
    sgu_dim = w_sgu_out.shape[0]
    x2 = h.reshape(bsz * seq, d)

    o_z = 3 * dn_dim
    o_a = o_z + dn_dim
    o_uv = o_a + 2 * DN_HEADS
    o_gate = o_uv + 2 * sgu_dim
    wqkv = w_in[:, :o_z].astype(BF16)
    wz = w_in[:, o_z:o_a].astype(BF16)
    wab = _pad_lanes(w_in[:, o_a:o_uv]).astype(BF16)
    wuv = w_in[:, o_uv:o_gate].astype(BF16)
    wgate = w_in[:, o_gate:].astype(BF16)
    gpar = jnp.stack([_pad_lanes(-jnp.exp(a_log.astype(F32))), _pad_lanes(dt_bias.astype(F32))])

    qkv, sz, gb = _proj_dn(x2, wqkv, wz, wab, conv_w, gpar, bsz=bsz, seq=seq, tm=PROJ_TILE)
    o, wg, wu, wd = _delta(qkv, sz, gb, dn_norm_g.reshape(1, -1), (expert_w_gate, expert_w_up, expert_w_down),
                           bsz=bsz, seq=seq, tc=DELTA_TILE)
    wr = _pad_lanes(jnp.concatenate([router_group_w, router_expert_w], axis=1))
    wr = jnp.concatenate(_split(wr), axis=1)
    br = _pad_lanes(jnp.concatenate([router_group_b, router_expert_b]).reshape(1, -1))
    h1, rinfo, dest, nsub = _merge(x2, o, wuv, wgate, sgu_ln_g.reshape(1, -1), sgu_ln_b.reshape(1, -1),
                                   spatial_w, spatial_b.T, w_sgu_out.astype(BF16),
                                   w_dn_out.astype(BF16), w_out.astype(BF16),
                                   ln1_g.reshape(1, -1), ln1_b.reshape(1, -1), wr, br,
                                   tm=MOE_BLOCK, alpha=alpha)
    h2 = _moe(h1, rinfo, dest, nsub, wg, wu, wd, ln2_g.reshape(1, -1), ln2_b.reshape(1, -1),
              tm=MOE_BLOCK, alpha=alpha)
    return h2.reshape(bsz, seq, d)


def kernel(x, w_in, conv_w, a_log, dt_bias, dn_norm_g, w_dn_out, sgu_ln_g, sgu_ln_b, spatial_w, spatial_b, w_sgu_out, w_out, ln1_g, ln1_b, router_group_w, router_group_b, router_expert_w, router_expert_b, expert_w_gate, expert_w_up, expert_w_down, ln2_g, ln2_b):
    depth = w_in.shape[0]
    alpha = (2.0 * depth) ** 0.25
    h = x
    for l in range(depth):
        h = _layer(h, w_in[l], conv_w[l], a_log[l], dt_bias[l], dn_norm_g[l], w_dn_out[l],
                   sgu_ln_g[l], sgu_ln_b[l], spatial_w[l], spatial_b[l], w_sgu_out[l], w_out[l],
                   ln1_g[l], ln1_b[l], router_group_w[l], router_group_b[l],
                   router_expert_w[l], router_expert_b[l],
                   expert_w_gate[l], expert_w_up[l], expert_w_down[l], ln2_g[l], ln2_b[l], alpha=alpha)
    return h
```

```python
import functools

import jax
import jax.numpy as jnp
from jax import lax
from jax.experimental import pallas as pl
from jax.experimental.pallas import tpu as pltpu

F32 = jnp.float32
BF16 = jnp.bfloat16

LANES = 128
DN_HEADS = 8
DN_HEAD_DIM = 128
CONV_WIDTH = 4
SGU_GROUPS = 8
SGU_CHUNK = 128
N_GROUPS = 4
EXPERTS_PER_GROUP = 4
LN_EPS = 1e-5
NORM_EPS = 1e-6
DELTA_CHUNK = 128
INVERSE_BASE_BLOCK = 32
CONV_PAD = 8
PROJ_TILE = 1024
DELTA_TILE = 512
MOE_BLOCK = 512
SUB_ROWS = 160
VMEM_LIMIT = 56 * 1024 * 1024

NT_DIMS = (((1,), (1,)), ((), ()))
TN_DIMS = (((0,), (0,)), ((), ()))


def _dot_f32(a, b):
    return jnp.dot(a, b, preferred_element_type=F32, precision=lax.Precision.HIGHEST)


def _split(x):
    hi = x.astype(BF16)
    return hi, (x - hi.astype(F32)).astype(BF16)


def _dot3_rows(lhs, b_hi, b_lo):
    m = lhs[0].shape[0]
    n = len(lhs)
    parts = [_split(a) for a in lhs]
    his = [hi for hi, _ in parts]
    los = [lo for _, lo in parts]
    r_hi = jnp.dot(jnp.concatenate(his + los, axis=0), b_hi, preferred_element_type=F32)
    r_lo = jnp.dot(jnp.concatenate(his, axis=0) if n > 1 else his[0], b_lo, preferred_element_type=F32)
    return [r_hi[i * m:(i + 1) * m] + r_hi[(n + i) * m:(n + i + 1) * m] + r_lo[i * m:(i + 1) * m]
            for i in range(n)]


def _sigmoid(x):
    return 0.5 + 0.5 * jnp.tanh(0.5 * x)


def _silu(x):
    half = 0.5 * x
    return half + half * jnp.tanh(half)


def _const_spec(shape):
    nd = len(shape)
    return pl.BlockSpec(shape, lambda *_: (0,) * nd, pipeline_mode=pl.Buffered(1))


def _params(n_axes):
    return pltpu.CompilerParams(dimension_semantics=("arbitrary",) * n_axes,
                                vmem_limit_bytes=VMEM_LIMIT)


def _proj_dn_kernel(x_ref, wqkv_ref, wz_ref, wab_ref, conv_ref, gpar_ref,
                    qkv_ref, sz_ref, gb_ref, ext_ref, *, tm, dn_dim):
    i = pl.program_id(1)
    xb = x_ref[...].astype(BF16)
    n_qkv = 3 * dn_dim

    @pl.when(i == 0)
    def _():
        ext_ref[0:CONV_PAD, :] = jnp.zeros((CONV_PAD, n_qkv), F32)

    for n in range(0, n_qkv, 256):
        ext_ref[CONV_PAD:CONV_PAD + tm, n:n + 256] = jnp.dot(
            xb, wqkv_ref[:, n:n + 256], preferred_element_type=F32)

    for n in range(0, n_qkv, LANES):
        w = conv_ref[:, n:n + LANES]
        y = w[3:4, :] * ext_ref[CONV_PAD:CONV_PAD + tm, n:n + LANES]
        for j in range(1, CONV_WIDTH):
            y = y + w[3 - j:4 - j, :] * ext_ref[CONV_PAD - j:CONV_PAD - j + tm, n:n + LANES]
        y = _silu(y)
        if n < 2 * dn_dim:
            y = y * lax.rsqrt(jnp.sum(y * y, axis=-1, keepdims=True) + NORM_EPS)
            if n < dn_dim:
                y = y * (DN_HEAD_DIM ** -0.5)
        qkv_ref[:, n:n + LANES] = y.astype(BF16)

    ext_ref[0:CONV_PAD, :] = ext_ref[tm:tm + CONV_PAD, :]

    for n in range(0, dn_dim, 256):
        z = jnp.dot(xb, wz_ref[:, n:n + 256], preferred_element_type=F32)
        sz_ref[:, n:n + 256] = _silu(z).astype(BF16)

    ab = jnp.dot(xb, wab_ref[...], preferred_element_type=F32)
    neg_exp_alog = gpar_ref[0:1, :]
    dt_bias = gpar_ref[1:2, :]
    sp_in = ab + dt_bias
    softplus = jnp.maximum(sp_in, 0.0) + jnp.log1p(jnp.exp(-jnp.abs(sp_in)))
    lane = lax.broadcasted_iota(jnp.int32, ab.shape, 1)
    gb_ref[...] = jnp.where(lane < DN_HEADS, neg_exp_alog * softplus, _sigmoid(ab))


def _proj_dn(x2, wqkv, wz, wab, conv_w, gpar, *, bsz, seq, tm):
    d = x2.shape[1]
    dn_dim = wz.shape[1]
    nl = seq // tm
    tok = lambda b, i: (b * nl + i, 0)
    return pl.pallas_call(
        functools.partial(_proj_dn_kernel, tm=tm, dn_dim=dn_dim),
        grid=(bsz, nl),
        in_specs=[pl.BlockSpec((tm, d), tok),
                  _const_spec(wqkv.shape), _const_spec(wz.shape), _const_spec(wab.shape),
                  _const_spec(conv_w.shape), _const_spec(gpar.shape)],
        out_specs=[pl.BlockSpec((tm, 3 * dn_dim), tok),
                   pl.BlockSpec((tm, dn_dim), tok),
                   pl.BlockSpec((tm, LANES), tok)],
        out_shape=[jax.ShapeDtypeStruct((bsz * seq, 3 * dn_dim), BF16),
                   jax.ShapeDtypeStruct((bsz * seq, dn_dim), BF16),
                   jax.ShapeDtypeStruct((bsz * seq, LANES), F32)],
        scratch_shapes=[pltpu.VMEM((tm + CONV_PAD, 3 * dn_dim), F32)],
        compiler_params=_params(2),
        name="proj_dn",
    )(x2, wqkv, wz, wab, conv_w, gpar)


def _delta_kernel(q_ref, k_ref, v_ref, gb_ref, sz_ref, ng_ref, *refs, tc, n_cast):
    f32_slabs, o_ref, bf16_slabs, state_ref = refs[:n_cast], refs[n_cast], refs[n_cast + 1:-1], refs[-1]
    c = DELTA_CHUNK
    hd = DN_HEAD_DIM
    pw = 2 * hd
    assert c == hd
    for src, dst in zip(f32_slabs, bf16_slabs, strict=True):
        dst[...] = src[...].astype(BF16)

    @pl.when(pl.program_id(1) == 0)
    def _():
        state_ref[...] = jnp.zeros(state_ref.shape, F32)

    row = lax.broadcasted_iota(jnp.int32, (c, pw), 0)
    lane = lax.broadcasted_iota(jnp.int32, (c, pw), 1)
    first = lane < hd
    col = jnp.where(first, lane, lane - hd)
    causal = row >= col
    strict = row > col
    row_s = lax.broadcasted_iota(jnp.int32, (pw, pw), 0)
    lane_s = lax.broadcasted_iota(jnp.int32, (pw, pw), 1)
    same_head = (row_s < hd) == (lane_s < hd)
    trow = lax.broadcasted_iota(jnp.int32, (c, c), 0)
    tcol = lax.broadcasted_iota(jnp.int32, (c, c), 1)
    tri = (trow >= tcol).astype(F32)
    norm_g = jnp.concatenate([ng_ref[...], ng_ref[...]], axis=1)

    def blockdiag(x):
        z = jnp.zeros_like(x)
        return jnp.concatenate([jnp.where(first, x, z), jnp.where(first, z, x)], axis=0)

    def pair_cols(a, j0, j1):
        return jnp.concatenate([jnp.broadcast_to(a[:, j0:j0 + 1], (c, hd)),
                                jnp.broadcast_to(a[:, j1:j1 + 1], (c, hd))], axis=1)

    def pair_rows(a, j0, j1, r):
        return jnp.concatenate([jnp.broadcast_to(a[r:r + 1, j0:j0 + 1], (1, hd)),
                                jnp.broadcast_to(a[r:r + 1, j1:j1 + 1], (1, hd))], axis=1)

    n_pairs = DN_HEADS // 2
    n_chunks = tc // c
    units = [(ci, p) for ci in range(n_chunks) for p in range(n_pairs)]

    gcums = []
    for ci in range(n_chunks):
        gb = gb_ref[ci * c:(ci + 1) * c, :]
        gcum = _dot_f32(tri, gb)
        gcums.append((gb, gcum, gcum.T))
    pre = {}
    for ci, p in units:
        gb, gcum, gcum_t = gcums[ci]
        h0, h1 = 2 * p, 2 * p + 1
        rows = slice(ci * c, (ci + 1) * c)
        cols = slice(p * pw, (p + 1) * pw)
        q = q_ref[rows, cols].astype(F32)
        k_b = k_ref[rows, cols]
        k = k_b.astype(F32)
        v = v_ref[rows, cols].astype(F32)
        g_col = pair_cols(gcum, h0, h1)
        g_row = jnp.concatenate([gcum_t[h0:h0 + 1, :], gcum_t[h1:h1 + 1, :]], axis=1)
        g_last = pair_rows(gcum, h0, h1, c - 1)
        beta = pair_cols(gb, DN_HEADS + h0, DN_HEADS + h1)
        decay = jnp.where(causal, jnp.exp(jnp.where(causal, g_col - g_row, 0.0)), 0.0)
        exp_g = jnp.exp(g_col)
        k_beta = k * beta
        aq = lax.dot_general(jnp.concatenate([k_beta, q], axis=0).astype(BF16), blockdiag(k_b),
                             NT_DIMS, preferred_element_type=F32)
        pre[ci, p] = dict(
            neg_a=jnp.where(strict, -(aq[:c] * decay), 0.0),
            qk=(aq[c:] * decay).astype(BF16),
            vb_bd=blockdiag((v * beta).astype(BF16)),
            kbe_bd=blockdiag((k_beta * exp_g).astype(BF16)),
            q_exp=(q * exp_g).astype(BF16),
            k_dec=(k * jnp.exp(g_last - g_col)).astype(BF16),
            exp_last=jnp.exp(g_last))

    base = INVERSE_BASE_BLOCK

    def masks(b):
        row_b = lax.broadcasted_iota(jnp.int32, (b, pw), 0)
        lane_b = lax.broadcasted_iota(jnp.int32, (b, pw), 1)
        blk_id = lax.shift_right_logical(lane_b, b.bit_length() - 1)
        blocks = [blk_id == r for r in range(pw // b)]
        eye_b = (row_b == (lane_b & (b - 1))).astype(F32)
        return blocks, eye_b

    def lane_blockdiag(x, blocks):
        z = jnp.zeros_like(x)
        return jnp.concatenate([jnp.where(m, x, z) for m in blocks], axis=0)

    def bd_parts(x, blocks):
        hi, lo = _split(x)
        return lane_blockdiag(hi, blocks), lane_blockdiag(lo, blocks)

    def diag_blocks(a, b):
        lane_2b = lax.broadcasted_iota(jnp.int32, (b, pw), 1)
        upper = (lane_2b & (2 * b - 1)) < b
        return jnp.where(upper, a[:b], a[b:])

    packed = {c: {u: pre[u]["neg_a"] for u in units}}
    size = c
    while size > base:
        packed[size // 2] = {u: diag_blocks(packed[size][u], size // 2) for u in units}
        size //= 2

    blocks, eye_b = masks(base)
    nil = dict(packed[base])
    t_blk = {u: eye_b + nil[u] for u in units}
    for u in units:
        nil[u], = _dot3_rows([nil[u]], *bd_parts(nil[u], blocks))
    levels = base.bit_length() - 2
    for lvl in range(levels):
        rhs = {u: bd_parts(nil[u], blocks) for u in units}
        for u in units:
            if lvl + 1 < levels:
                t_inc, nil[u] = _dot3_rows([t_blk[u], nil[u]], *rhs[u])
            else:
                t_inc, = _dot3_rows([t_blk[u]], *rhs[u])
            t_blk[u] = t_blk[u] + t_inc

    size = base
    while size < c:
        blocks, _ = masks(size)
        lane_s = lax.broadcasted_iota(jnp.int32, (size, pw), 1)
        left = (lane_s & (2 * size - 1)) < size
        x21 = {}
        for u in units:
            n21 = jnp.where(left, packed[2 * size][u][size:], 0.0)
            x21[u], = _dot3_rows([n21], *bd_parts(t_blk[u], blocks))
        grown = {}
        for u in units:
            x_hi, x_lo = _split(x21[u])
            z = jnp.zeros_like(x_hi)
            rhs_hi = jnp.concatenate([jnp.where(blocks[r - 1], x_hi, z) if r % 2 else z
                                      for r in range(pw // size)], axis=0)
            rhs_lo = jnp.concatenate([jnp.where(blocks[r - 1], x_lo, z) if r % 2 else z
                                      for r in range(pw // size)], axis=0)
            t21, = _dot3_rows([t_blk[u]], rhs_hi, rhs_lo)
            grown[u] = jnp.concatenate([jnp.where(left, t_blk[u], 0.0),
                                        jnp.where(left, t21, t_blk[u])], axis=0)
        t_blk = grown
        size *= 2
    t_inv = t_blk
    u_mat, w_mat = {}, {}
    for u in units:
        t_b = t_inv[u].astype(BF16)
        u_mat[u] = jnp.dot(t_b, pre[u]["vb_bd"], preferred_element_type=F32)
        w_mat[u] = jnp.dot(t_b, pre[u]["kbe_bd"], preferred_element_type=F32).astype(BF16)

    states = [state_ref[p] for p in range(n_pairs)]
    for ci in range(n_chunks):
        rows = slice(ci * c, (ci + 1) * c)
        for p in range(n_pairs):
            u = (ci, p)
            cols = slice(p * pw, (p + 1) * pw)
            state = states[p]
            ws_qs = jnp.dot(jnp.concatenate([w_mat[u], pre[u]["q_exp"]], axis=0), state.astype(BF16),
                            preferred_element_type=F32)
            v_new_b = (u_mat[u] - ws_qs[:c]).astype(BF16)
            out = ws_qs[c:] + jnp.dot(pre[u]["qk"], blockdiag(v_new_b), preferred_element_type=F32)
            kv = lax.dot_general(pre[u]["k_dec"], v_new_b, TN_DIMS, preferred_element_type=F32)
            states[p] = jnp.where(same_head, state * pre[u]["exp_last"] + kv, 0.0)
            sq = out * out
            ms = jnp.concatenate(
                [jnp.broadcast_to(jnp.mean(sq[:, :hd], axis=-1, keepdims=True), (c, hd)),
                 jnp.broadcast_to(jnp.mean(sq[:, hd:], axis=-1, keepdims=True), (c, hd))], axis=1)
            out = out * lax.rsqrt(ms + NORM_EPS) * norm_g * sz_ref[rows, cols].astype(F32)
            o_ref[rows, cols] = out.astype(BF16)
    for p in range(n_pairs):
        state_ref[p] = states[p]


def _delta(qkv, sz, gb, norm_g, cast_weights, *, bsz, seq, tc):
    dn_dim = sz.shape[1]
    nl = seq // tc
    tok = lambda b, i: (b * nl + i, 0)

    def slab_spec(w):
        slabs = (bsz * nl) // w.shape[0]
        assert slabs * w.shape[0] == bsz * nl and w.shape[1] % (16 * slabs) == 0
        return pl.BlockSpec((1, w.shape[1] // slabs, w.shape[2]),
                            lambda b, i: ((b * nl + i) // slabs, (b * nl + i) % slabs, 0))

    return pl.pallas_call(
        functools.partial(_delta_kernel, tc=tc, n_cast=len(cast_weights)),
        grid=(bsz, nl),
        in_specs=[pl.BlockSpec((tc, dn_dim), lambda b, i: (b * nl + i, 0)),
                  pl.BlockSpec((tc, dn_dim), lambda b, i: (b * nl + i, 1)),
                  pl.BlockSpec((tc, dn_dim), lambda b, i: (b * nl + i, 2)),
                  pl.BlockSpec((tc, LANES), tok),
                  pl.BlockSpec((tc, dn_dim), tok),
                  _const_spec(norm_g.shape)] + [slab_spec(w) for w in cast_weights],
        out_specs=[pl.BlockSpec((tc, dn_dim), tok)] + [slab_spec(w) for w in cast_weights],
        out_shape=[jax.ShapeDtypeStruct((bsz * seq, dn_dim), BF16)]
        + [jax.ShapeDtypeStruct(w.shape, BF16) for w in cast_weights],
        scratch_shapes=[pltpu.VMEM((DN_HEADS // 2, 2 * DN_HEAD_DIM, 2 * DN_HEAD_DIM), F32)],
        compiler_params=_params(2),
        name="delta",
    )(qkv, qkv, qkv, gb, sz, norm_g, *cast_weights)


def _sgu_body(x_ref, wuv_ref, wgate_ref, lng_ref, lnb_ref, sw_ref, sb_ref, wso_ref, u_ref, v_ref, gated_ref,
              *, tm, sgu_dim, d_model):
    xb = x_ref[...].astype(BF16)
    inv_sqrt2 = 2.0 ** -0.5
    for n in range(0, 2 * sgu_dim, 256):
        uv = jnp.dot(xb, wuv_ref[:, n:n + 256], preferred_element_type=F32)
        act = 0.5 * uv * (1.0 + lax.erf(uv * inv_sqrt2))
        if n < sgu_dim:
            u_ref[:, n:n + 256] = act
        else:
            v_ref[:, n - sgu_dim:n - sgu_dim + 256] = act

    v = v_ref[...]
    vc = v - jnp.mean(v, axis=-1, keepdims=True)
    var = jnp.mean(vc * vc, axis=-1, keepdims=True)
    v_ref[...] = vc * lax.rsqrt(var + LN_EPS) * lng_ref[...] + lnb_ref[...]

    row = lax.broadcasted_iota(jnp.int32, (SGU_CHUNK, SGU_CHUNK), 0)
    col = lax.broadcasted_iota(jnp.int32, (SGU_CHUNK, SGU_CHUNK), 1)
    causal = row >= col
    for g in range(SGU_GROUPS):
        cols = slice(g * LANES, (g + 1) * LANES)
        w_causal = jnp.where(causal, sw_ref[g], 0.0).astype(BF16)
        bias = sb_ref[:, g:g + 1]
        for ci in range(tm // SGU_CHUNK):
            rows = slice(ci * SGU_CHUNK, (ci + 1) * SGU_CHUNK)
            mixed = jnp.dot(w_causal, v_ref[rows, cols].astype(BF16), preferred_element_type=F32) + bias
            gated_ref[rows, cols] = (u_ref[rows, cols] * mixed).astype(BF16)

    gated = gated_ref[...]
    ys, sgd = [], []
    for n in range(0, d_model, 256):
        y_sgu = jnp.dot(gated, wso_ref[:, n:n + 256], preferred_element_type=F32)
        gate_sgu = jnp.dot(xb, wgate_ref[:, d_model + n:d_model + n + 256], preferred_element_type=F32)
        ys.append(_sigmoid(gate_sgu) * y_sgu)
        gate_dn = jnp.dot(xb, wgate_ref[:, n:n + 256], preferred_element_type=F32)
        sgd.append(_sigmoid(gate_dn))
    return jnp.concatenate(ys, axis=1), jnp.concatenate(sgd, axis=1)


def _layer_norm(r, g, b):
    rc = r - jnp.mean(r, axis=-1, keepdims=True)
    var = jnp.mean(rc * rc, axis=-1, keepdims=True)
    return rc * lax.rsqrt(var + LN_EPS) * g + b


def _merge_kernel(x_ref, o_ref, wuv_ref, wgate_ref, lng_ref, lnb_ref, sw_ref, sb_ref, wso_ref,
                  wdn_ref, wout_ref, g_ref, b_ref, wr_ref, br_ref,
                  h_ref, rinfo_ref, dest_ref, nsub_ref, info_ref, u_ref, v_ref, gated_ref,
                  *, alpha, tm, sgu_dim, d_model):
    ys, sgd = _sgu_body(x_ref, wuv_ref, wgate_ref, lng_ref, lnb_ref, sw_ref, sb_ref, wso_ref,
                        u_ref, v_ref, gated_ref, tm=tm, sgu_dim=sgu_dim, d_model=d_model)
    y_dn = jnp.dot(o_ref[...], wdn_ref[...], preferred_element_type=F32)
    y = sgd * y_dn + ys
    mix = jnp.dot(y.astype(BF16), wout_ref[...], preferred_element_type=F32)
    h = _layer_norm(alpha * x_ref[...] + mix, g_ref[...], b_ref[...])
    h_ref[...] = h

    h_hi, h_lo = _split(h)
    parts = jnp.dot(jnp.concatenate([h_hi, h_lo], axis=0), wr_ref[...], preferred_element_type=F32)
    logits = (parts[:tm, :LANES] + parts[tm:, :LANES]) + (parts[:tm, LANES:] + parts[tm:, LANES:]) + br_ref[...]
    lt = logits.T
    row_of = lambda i: lt[i:i + 1, :]
    g_logit = [row_of(g) for g in range(N_GROUPS)]
    g_max = functools.reduce(jnp.maximum, g_logit)
    g_idx = jnp.full(g_max.shape, N_GROUPS - 1, jnp.int32)
    for g in range(N_GROUPS - 2, -1, -1):
        g_idx = jnp.where(g_logit[g] == g_max, g, g_idx)
    group_p = 1.0 / functools.reduce(jnp.add, [jnp.exp(gl - g_max) for gl in g_logit])
    e_logit = []
    for j in range(EXPERTS_PER_GROUP):
        sel = row_of(N_GROUPS + (N_GROUPS - 1) * EXPERTS_PER_GROUP + j)
        for g in range(N_GROUPS - 2, -1, -1):
            sel = jnp.where(g_idx == g, row_of(N_GROUPS + g * EXPERTS_PER_GROUP + j), sel)
        e_logit.append(sel)
    e1 = functools.reduce(jnp.maximum, e_logit)
    j1 = jnp.full(e1.shape, EXPERTS_PER_GROUP - 1, jnp.int32)
    for j in range(EXPERTS_PER_GROUP - 2, -1, -1):
        j1 = jnp.where(e_logit[j] == e1, j, j1)
    rest = [jnp.where(j1 == j, -jnp.inf, e_logit[j]) for j in range(EXPERTS_PER_GROUP)]
    e2 = functools.reduce(jnp.maximum, rest)
    j2 = jnp.full(e2.shape, EXPERTS_PER_GROUP - 1, jnp.int32)
    for j in range(EXPERTS_PER_GROUP - 2, -1, -1):
        j2 = jnp.where(rest[j] == e2, j, j2)
    r = jnp.exp(e2 - e1)
    p1 = group_p / (1.0 + r)
    p2 = group_p * r / (1.0 + r)
    p1_hi = p1.astype(BF16).astype(F32)
    p2_hi = p2.astype(BF16).astype(F32)

    onehot = [jnp.where(g_idx == g, 1.0, 0.0) for g in range(N_GROUPS)]
    trow = lax.broadcasted_iota(jnp.int32, (tm, tm), 0)
    tcol = lax.broadcasted_iota(jnp.int32, (tm, tm), 1)
    earlier = jnp.where(trow < tcol, 1.0, 0.0).astype(BF16)
    onehot_rows = jnp.concatenate(onehot + [jnp.zeros((8 - N_GROUPS, tm), F32)], axis=0).astype(BF16)
    rank = jnp.dot(onehot_rows, earlier, preferred_element_type=F32)
    lane1 = lax.broadcasted_iota(jnp.int32, (1, LANES), 1)
    nsub_row = jnp.zeros((1, LANES), F32)
    dest_row = jnp.zeros((1, tm), F32)
    start = jnp.zeros((1, 1), F32)
    for g in range(N_GROUPS):
        count = jnp.sum(onehot[g], axis=-1, keepdims=True)
        n_sub = jnp.floor((count + (SUB_ROWS - 1)) * (1.0 / SUB_ROWS))
        dest_row = dest_row + onehot[g] * (start * SUB_ROWS + rank[g:g + 1, :])
        nsub_row = jnp.where(lane1 == g, n_sub, jnp.where(lane1 == N_GROUPS + g, start, nsub_row))
        start = start + n_sub
    dest_ref[0] = dest_row.astype(jnp.int32)
    nsub_ref[0] = nsub_row.astype(jnp.int32)
    info_ref[...] = jnp.zeros(info_ref.shape, F32)
    for j in range(EXPERTS_PER_GROUP):
        info_ref[j:j + 1, :] = jnp.where(j1 == j, p1_hi, jnp.where(j2 == j, p2_hi, 0.0))
        info_ref[EXPERTS_PER_GROUP + j:EXPERTS_PER_GROUP + j + 1, :] = jnp.where(
            j1 == j, p1 - p1_hi, jnp.where(j2 == j, p2 - p2_hi, 0.0))
    info_ref[2 * EXPERTS_PER_GROUP:2 * EXPERTS_PER_GROUP + 1, :] = dest_row
    rinfo_ref[...] = info_ref[...].T


def _merge(x2, o, wuv, wgate, sgu_ln_g, sgu_ln_b, spatial_w, spatial_b_t, wso, wdn, wout, ln_g, ln_b, wr, br,
           *, tm, alpha):
    t, d = x2.shape
    sgu_dim = wso.shape[0]
    nb = t // tm
    tok = lambda i: (i, 0)
    consts = (wuv, wgate, sgu_ln_g, sgu_ln_b, spatial_w, spatial_b_t, wso, wdn, wout, ln_g, ln_b, wr, br)
    return pl.pallas_call(
        functools.partial(_merge_kernel, alpha=alpha, tm=tm, sgu_dim=sgu_dim, d_model=d),
        grid=(nb,),
        in_specs=[pl.BlockSpec((tm, d), tok), pl.BlockSpec((tm, o.shape[1]), tok)]
        + [_const_spec(c.shape) for c in consts],
        out_specs=[pl.BlockSpec((tm, d), tok), pl.BlockSpec((tm, LANES), tok),
                   pl.BlockSpec((1, 1, tm), lambda i: (i, 0, 0)),
                   pl.BlockSpec((1, 1, LANES), lambda i: (i, 0, 0))],
        out_shape=[jax.ShapeDtypeStruct((t, d), F32), jax.ShapeDtypeStruct((t, LANES), F32),
                   jax.ShapeDtypeStruct((nb, 1, tm), jnp.int32),
                   jax.ShapeDtypeStruct((nb, 1, LANES), jnp.int32)],
        scratch_shapes=[pltpu.VMEM((LANES, tm), F32),
                        pltpu.VMEM((tm, sgu_dim), F32), pltpu.VMEM((tm, sgu_dim), F32),
                        pltpu.VMEM((tm, sgu_dim), BF16)],
        compiler_params=_params(1),
        name="sgu_merge",
    )(x2, o, *consts)


def _moe_kernel(nsub_ref, h_ref, rinfo_ref, dest_ref, wg_ref, wu_ref, wd_ref, g_ref, b_ref, out_ref, acc_ref,
                *, alpha, tm):
    blk = pl.program_id(0)
    h = h_ref[...]
    hb = h.astype(BF16)
    rinfo = rinfo_ref[...]
    lane = lax.broadcasted_iota(jnp.int32, rinfo.shape, 1)
    cw_parts = jnp.where(lane < 2 * EXPERTS_PER_GROUP, rinfo, 0.0).astype(BF16)
    dest_col = rinfo[:, 2 * EXPERTS_PER_GROUP:2 * EXPERTS_PER_GROUP + 1].astype(jnp.int32)
    dest_row = dest_ref[0]
    sub_row = lax.broadcasted_iota(jnp.int32, (SUB_ROWS, tm), 0)
    sub_lane = lax.broadcasted_iota(jnp.int32, (tm, SUB_ROWS), 1)

    def sub_tiles(groups, first_rows):
        n = len(groups)
        gather = [jnp.where(sub_row + r == dest_row, 1.0, 0.0).astype(BF16) for r in first_rows]
        scatter = [jnp.where(sub_lane + r == dest_col, 1.0, 0.0).astype(BF16) for r in first_rows]
        xs = [jnp.dot(gather[i], hb, preferred_element_type=F32).astype(BF16) for i in range(n)]
        cws = [jnp.dot(gather[i], cw_parts, preferred_element_type=F32) for i in range(n)]
        y = [None] * n
        for j in range(EXPERTS_PER_GROUP):
            hg = [jnp.dot(xs[i], wg_ref[groups[i] * EXPERTS_PER_GROUP + j], preferred_element_type=F32)
                  for i in range(n)]
            hu = [jnp.dot(xs[i], wu_ref[groups[i] * EXPERTS_PER_GROUP + j], preferred_element_type=F32)
                  for i in range(n)]
            for i in range(n):
                c_e = cws[i][:, j:j + 1] + cws[i][:, EXPERTS_PER_GROUP + j:EXPERTS_PER_GROUP + j + 1]
                hm = (_silu(hg[i]) * hu[i] * c_e).astype(BF16)
                part = jnp.dot(hm, wd_ref[groups[i] * EXPERTS_PER_GROUP + j], preferred_element_type=F32)
                y[i] = part if y[i] is None else y[i] + part
        out = [jnp.dot(scatter[i], y[i].astype(BF16), preferred_element_type=F32) for i in range(n)]
        return functools.reduce(jnp.add, out)

    n_sub = [nsub_ref[blk * LANES + g] for g in range(N_GROUPS)]
    first = [nsub_ref[blk * LANES + N_GROUPS + g] for g in range(N_GROUPS)]
    past_end = (tm // SUB_ROWS + N_GROUPS) * SUB_ROWS
    acc_ref[...] = sub_tiles(list(range(N_GROUPS)),
                             [jnp.where(n_sub[g] > 0, first[g] * SUB_ROWS, past_end) for g in range(N_GROUPS)])

    for g in range(N_GROUPS):
        def later_sub_tile(s, carry, g=g):
            acc_ref[...] += sub_tiles([g], [(first[g] + s) * SUB_ROWS])
            return carry

        lax.fori_loop(1, n_sub[g], later_sub_tile, 0)

    out_ref[...] = _layer_norm(alpha * h + acc_ref[...], g_ref[...], b_ref[...])


def _moe(h1, rinfo, dest, nsub, wg, wu, wd, ln_g, ln_b, *, tm, alpha):
    t, d = h1.shape
    nd = lambda shape: pl.BlockSpec(shape, lambda i, ns: (0,) * len(shape), pipeline_mode=pl.Buffered(1))
    tok = lambda i, ns: (i, 0)
    grid_spec = pltpu.PrefetchScalarGridSpec(
        num_scalar_prefetch=1,
        grid=(t // tm,),
        in_specs=[pl.BlockSpec((tm, d), tok), pl.BlockSpec((tm, LANES), tok),
                  pl.BlockSpec((1, 1, tm), lambda i, ns: (i, 0, 0)),
                  nd(wg.shape), nd(wu.shape), nd(wd.shape), nd(ln_g.shape), nd(ln_b.shape)],
        out_specs=pl.BlockSpec((tm, d), tok),
        scratch_shapes=[pltpu.VMEM((tm, d), F32)])
    return pl.pallas_call(
        functools.partial(_moe_kernel, alpha=alpha, tm=tm),
        grid_spec=grid_spec,
        out_shape=jax.ShapeDtypeStruct((t, d), F32),
        compiler_params=_params(1),
        name="moe",
    )(nsub.reshape(-1), h1, rinfo, dest, wg, wu, wd, ln_g, ln_b)


def _pad_lanes(a, width=LANES):
    return jnp.pad(a, [(0, 0)] * (a.ndim - 1) + [(0, width - a.shape[-1])])


def _layer(h, w_in, conv_w, a_log, dt_bias, dn_norm_g, w_dn_out, sgu_ln_g, sgu_ln_b,
           spatial_w, spatial_b, w_sgu_out, w_out, ln1_g, ln1_b,
           router_group_w, router_group_b, router_expert_w, router_expert_b,
           expert_w_gate, expert_w_up, expert_w_down, ln2_g, ln2_b, *, alpha):
    bsz, seq, d = h.shape
    dn_dim = w_dn_out.shape[0]
    sgu_dim = w_sgu_out.shape[0]
    x2 = h.reshape(bsz * seq, d)

    o_z = 3 * dn_dim
    o_a = o_z + dn_dim
    o_uv = o_a + 2 * DN_HEADS
    o_gate = o_uv + 2 * sgu_dim
    wqkv = w_in[:, :o_z].astype(BF16)
    wz = w_in[:, o_z:o_a].astype(BF16)
    wab = _pad_lanes(w_in[:, o_a:o_uv]).astype(BF16)
    gpar = jnp.stack([_pad_lanes(-jnp.exp(a_log.astype(F32))), _pad_lanes(dt_bias.astype(F32))])

    qkv, sz, gb = _proj_dn(x2, wqkv, wz, wab, conv_w, gpar, bsz=bsz, seq=seq, tm=PROJ_TILE)
    o, wg, wu, wd, w_in_b = _delta(qkv, sz, gb, dn_norm_g.reshape(1, -1),
                                   (expert_w_gate, expert_w_up, expert_w_down, w_in[None]),
                                   bsz=bsz, seq=seq, tc=DELTA_TILE)
    wuv = w_in_b[0, :, o_uv:o_gate]
    wgate = w_in_b[0, :, o_gate:]
    wr = _pad_lanes(jnp.concatenate([router_group_w, router_expert_w], axis=1))
    wr = jnp.concatenate(_split(wr), axis=1)
    br = _pad_lanes(jnp.concatenate([router_group_b, router_expert_b]).reshape(1, -1))
    h1, rinfo, dest, nsub = _merge(x2, o, wuv, wgate, sgu_ln_g.reshape(1, -1), sgu_ln_b.reshape(1, -1),
                                   spatial_w, spatial_b.T, w_sgu_out.astype(BF16),
                                   w_dn_out.astype(BF16), w_out.astype(BF16),
                                   ln1_g.reshape(1, -1), ln1_b.reshape(1, -1), wr, br,
                                   tm=MOE_BLOCK, alpha=alpha)
    h2 = _moe(h1, rinfo, dest, nsub, wg, wu, wd, ln2_g.reshape(1, -1), ln2_b.reshape(1, -1),
              tm=MOE_BLOCK, alpha=alpha)
    return h2.reshape(bsz, seq, d)


def kernel(x, w_in, conv_w, a_log, dt_bias, dn_norm_g, w_dn_out, sgu_ln_g, sgu_ln_b, spatial_w, spatial_b, w_sgu_out, w_out, ln1_g, ln1_b, router_group_w, router_group_b, router_expert_w, router_expert_b, expert_w_gate, expert_w_up, expert_w_down, ln2_g, ln2_b):
    depth = w_in.shape[0]
    alpha = (2.0 * depth) ** 0.25
    h = x
    for l in range(depth):
        h = _layer(h, w_in[l], conv_w[l], a_log[l], dt_bias[l], dn_norm_g[l], w_dn_out[l],
                   sgu_ln_g[l], sgu_ln_b[l], spatial_w[l], spatial_b[l], w_sgu_out[l], w_out[l],
                   ln1_g[l], ln1_b[l], router_group_w[l], router_group_b[l],
                   router_expert_w[l], router_expert_b[l],
                   expert_w_gate[l], expert_w_up[l], expert_w_down[l], ln2_g[l], ln2_b[l], alpha=alpha)
    return h
```

```python
import functools

import jax
import jax.numpy as jnp
from jax import lax
from jax.experimental import pallas as pl
from jax.experimental.pallas import tpu as pltpu

F32 = jnp.float32
BF16 = jnp.bfloat16

LANES = 128
DN_HEADS = 8
DN_HEAD_DIM = 128
CONV_WIDTH = 4
SGU_GROUPS = 8
SGU_CHUNK = 128
N_GROUPS = 4
EXPERTS_PER_GROUP = 4
LN_EPS = 1e-5
NORM_EPS = 1e-6
DELTA_CHUNK = 128
INVERSE_BASE_BLOCK = 32
CONV_PAD = 8
PROJ_TILE = 1024
DELTA_TILE = 512
MOE_BLOCK = 512
SUB_ROWS = 160
VMEM_LIMIT = 56 * 1024 * 1024

NT_DIMS = (((1,), (1,)), ((), ()))
TN_DIMS = (((0,), (0,)), ((), ()))


def _dot_f32(a, b):
    return jnp.dot(a, b, preferred_element_type=F32, precision=lax.Precision.HIGHEST)


def _split(x):
    hi = x.astype(BF16)
    return hi, (x - hi.astype(F32)).astype(BF16)


def _dot3_rows(lhs, b_hi, b_lo):
    m = lhs[0].shape[0]
    n = len(lhs)
    parts = [_split(a) for a in lhs]
    his = [hi for hi, _ in parts]
    los = [lo for _, lo in parts]
    r_hi = jnp.dot(jnp.concatenate(his + los, axis=0), b_hi, preferred_element_type=F32)
    r_lo = jnp.dot(jnp.concatenate(his, axis=0) if n > 1 else his[0], b_lo, preferred_element_type=F32)
    return [r_hi[i * m:(i + 1) * m] + r_hi[(n + i) * m:(n + i + 1) * m] + r_lo[i * m:(i + 1) * m]
            for i in range(n)]


def _sigmoid(x):
    return 0.5 + 0.5 * jnp.tanh(0.5 * x)


def _silu(x):
    half = 0.5 * x
    return half + half * jnp.tanh(half)


def _const_spec(shape):
    nd = len(shape)
    return pl.BlockSpec(shape, lambda *_: (0,) * nd, pipeline_mode=pl.Buffered(1))


def _params(n_axes):
    return pltpu.CompilerParams(dimension_semantics=("arbitrary",) * n_axes,
                                vmem_limit_bytes=VMEM_LIMIT)


def _proj_dn_kernel(x_ref, wqkv_ref, wz_ref, wab_ref, conv_ref, gpar_ref,
                    qkv_ref, sz_ref, gb_ref, ext_ref, *, tm, dn_dim):
    i = pl.program_id(1)
    xb = x_ref[...].astype(BF16)
    n_qkv = 3 * dn_dim

    @pl.when(i == 0)
    def _():
        ext_ref[0:CONV_PAD, :] = jnp.zeros((CONV_PAD, n_qkv), F32)

    for n in range(0, n_qkv, 256):
        ext_ref[CONV_PAD:CONV_PAD + tm, n:n + 256] = jnp.dot(
            xb, wqkv_ref[:, n:n + 256], preferred_element_type=F32)

    for n in range(0, n_qkv, LANES):
        w = conv_ref[:, n:n + LANES]
        y = w[3:4, :] * ext_ref[CONV_PAD:CONV_PAD + tm, n:n + LANES]
        for j in range(1, CONV_WIDTH):
            y = y + w[3 - j:4 - j, :] * ext_ref[CONV_PAD - j:CONV_PAD - j + tm, n:n + LANES]
        y = _silu(y)
        if n < 2 * dn_dim:
            y = y * lax.rsqrt(jnp.sum(y * y, axis=-1, keepdims=True) + NORM_EPS)
            if n < dn_dim:
                y = y * (DN_HEAD_DIM ** -0.5)
        qkv_ref[:, n:n + LANES] = y.astype(BF16)

    ext_ref[0:CONV_PAD, :] = ext_ref[tm:tm + CONV_PAD, :]

    for n in range(0, dn_dim, 256):
        z = jnp.dot(xb, wz_ref[:, n:n + 256], preferred_element_type=F32)
        sz_ref[:, n:n + 256] = _silu(z).astype(BF16)

    ab = jnp.dot(xb, wab_ref[...], preferred_element_type=F32)
    neg_exp_alog = gpar_ref[0:1, :]
    dt_bias = gpar_ref[1:2, :]
    sp_in = ab + dt_bias
    softplus = jnp.maximum(sp_in, 0.0) + jnp.log1p(jnp.exp(-jnp.abs(sp_in)))
    lane = lax.broadcasted_iota(jnp.int32, ab.shape, 1)
    gb_ref[...] = jnp.where(lane < DN_HEADS, neg_exp_alog * softplus, _sigmoid(ab))


def _proj_dn(x2, wqkv, wz, wab, conv_w, gpar, *, bsz, seq, tm):
    d = x2.shape[1]
    dn_dim = wz.shape[1]
    nl = seq // tm
    tok = lambda b, i: (b * nl + i, 0)
    return pl.pallas_call(
        functools.partial(_proj_dn_kernel, tm=tm, dn_dim=dn_dim),
        grid=(bsz, nl),
        in_specs=[pl.BlockSpec((tm, d), tok),
                  _const_spec(wqkv.shape), _const_spec(wz.shape), _const_spec(wab.shape),
                  _const_spec(conv_w.shape), _const_spec(gpar.shape)],
        out_specs=[pl.BlockSpec((tm, 3 * dn_dim), tok),
                   pl.BlockSpec((tm, dn_dim), tok),
                   pl.BlockSpec((tm, LANES), tok)],
        out_shape=[jax.ShapeDtypeStruct((bsz * seq, 3 * dn_dim), BF16),
                   jax.ShapeDtypeStruct((bsz * seq, dn_dim), BF16),
                   jax.ShapeDtypeStruct((bsz * seq, LANES), F32)],
        scratch_shapes=[pltpu.VMEM((tm + CONV_PAD, 3 * dn_dim), F32)],
        compiler_params=_params(2),
        name="proj_dn",
    )(x2, wqkv, wz, wab, conv_w, gpar)


def _delta_kernel(q_ref, k_ref, v_ref, gb_ref, sz_ref, ng_ref, wg32_ref, wu32_ref, wd32_ref,
                  o_ref, wg_ref, wu_ref, wd_ref, state_ref, *, tc):
    c = DELTA_CHUNK
    hd = DN_HEAD_DIM
    pw = 2 * hd
    assert c == hd
    wg_ref[...] = wg32_ref[...].astype(BF16)
    wu_ref[...] = wu32_ref[...].astype(BF16)
    wd_ref[...] = wd32_ref[...].astype(BF16)

    @pl.when(pl.program_id(1) == 0)
    def _():
        state_ref[...] = jnp.zeros(state_ref.shape, F32)

    row = lax.broadcasted_iota(jnp.int32, (c, pw), 0)
    lane = lax.broadcasted_iota(jnp.int32, (c, pw), 1)
    first = lane < hd
    col = jnp.where(first, lane, lane - hd)
    causal = row >= col
    strict = row > col
    row_s = lax.broadcasted_iota(jnp.int32, (pw, pw), 0)
    lane_s = lax.broadcasted_iota(jnp.int32, (pw, pw), 1)
    same_head = (row_s < hd) == (lane_s < hd)
    trow = lax.broadcasted_iota(jnp.int32, (c, c), 0)
    tcol = lax.broadcasted_iota(jnp.int32, (c, c), 1)
    tri = (trow >= tcol).astype(F32)
    norm_g = jnp.concatenate([ng_ref[...], ng_ref[...]], axis=1)

    def blockdiag(x):
        z = jnp.zeros_like(x)
        return jnp.concatenate([jnp.where(first, x, z), jnp.where(first, z, x)], axis=0)

    def pair_cols(a, j0, j1):
        return jnp.concatenate([jnp.broadcast_to(a[:, j0:j0 + 1], (c, hd)),
                                jnp.broadcast_to(a[:, j1:j1 + 1], (c, hd))], axis=1)

    def pair_rows(a, j0, j1, r):
        return jnp.concatenate([jnp.broadcast_to(a[r:r + 1, j0:j0 + 1], (1, hd)),
                                jnp.broadcast_to(a[r:r + 1, j1:j1 + 1], (1, hd))], axis=1)

    n_pairs = DN_HEADS // 2
    n_chunks = tc // c
    units = [(ci, p) for ci in range(n_chunks) for p in range(n_pairs)]

    gcums = []
    for ci in range(n_chunks):
        gb = gb_ref[ci * c:(ci + 1) * c, :]
        gcum = _dot_f32(tri, gb)
        gcums.append((gb, gcum, gcum.T))
    pre = {}
    for ci, p in units:
        gb, gcum, gcum_t = gcums[ci]
        h0, h1 = 2 * p, 2 * p + 1
        rows = slice(ci * c, (ci + 1) * c)
        cols = slice(p * pw, (p + 1) * pw)
        q = q_ref[rows, cols].astype(F32)
        k_b = k_ref[rows, cols]
        k = k_b.astype(F32)
        v = v_ref[rows, cols].astype(F32)
        g_col = pair_cols(gcum, h0, h1)
        g_row = jnp.concatenate([gcum_t[h0:h0 + 1, :], gcum_t[h1:h1 + 1, :]], axis=1)
        g_last = pair_rows(gcum, h0, h1, c - 1)
        beta = pair_cols(gb, DN_HEADS + h0, DN_HEADS + h1)
        decay = jnp.where(causal, jnp.exp(jnp.where(causal, g_col - g_row, 0.0)), 0.0)
        exp_g = jnp.exp(g_col)
        k_beta = k * beta
        aq = lax.dot_general(jnp.concatenate([k_beta, q], axis=0).astype(BF16), blockdiag(k_b),
                             NT_DIMS, preferred_element_type=F32)
        pre[ci, p] = dict(
            neg_a=jnp.where(strict, -(aq[:c] * decay), 0.0),
            qk=(aq[c:] * decay).astype(BF16),
            vb_bd=blockdiag((v * beta).astype(BF16)),
            kbe_bd=blockdiag((k_beta * exp_g).astype(BF16)),
            q_exp=(q * exp_g).astype(BF16),
            k_dec=(k * jnp.exp(g_last - g_col)).astype(BF16),
            exp_last=jnp.exp(g_last))

    base = INVERSE_BASE_BLOCK

    def masks(b):
        row_b = lax.broadcasted_iota(jnp.int32, (b, pw), 0)
        lane_b = lax.broadcasted_iota(jnp.int32, (b, pw), 1)
        blk_id = lax.shift_right_logical(lane_b, b.bit_length() - 1)
        blocks = [blk_id == r for r in range(pw // b)]
        eye_b = (row_b == (lane_b & (b - 1))).astype(F32)
        return blocks, eye_b

    def lane_blockdiag(x, blocks):
        z = jnp.zeros_like(x)
        return jnp.concatenate([jnp.where(m, x, z) for m in blocks], axis=0)

    def bd_parts(x, blocks):
        hi, lo = _split(x)
        return lane_blockdiag(hi, blocks), lane_blockdiag(lo, blocks)

    def diag_blocks(a, b):
        lane_2b = lax.broadcasted_iota(jnp.int32, (b, pw), 1)
        upper = (lane_2b & (2 * b - 1)) < b
        return jnp.where(upper, a[:b], a[b:])

    packed = {c: {u: pre[u]["neg_a"] for u in units}}
    size = c
    while size > base:
        packed[size // 2] = {u: diag_blocks(packed[size][u], size // 2) for u in units}
        size //= 2

    blocks, eye_b = masks(base)
    nil = dict(packed[base])
    t_blk = {u: eye_b + nil[u] for u in units}
    for u in units:
        nil[u], = _dot3_rows([nil[u]], *bd_parts(nil[u], blocks))
    levels = base.bit_length() - 2
    for lvl in range(levels):
        rhs = {u: bd_parts(nil[u], blocks) for u in units}
        for u in units:
            if lvl + 1 < levels:
                t_inc, nil[u] = _dot3_rows([t_blk[u], nil[u]], *rhs[u])
            else:
                t_inc, = _dot3_rows([t_blk[u]], *rhs[u])
            t_blk[u] = t_blk[u] + t_inc

    size = base
    while size < c:
        blocks, _ = masks(size)
        lane_s = lax.broadcasted_iota(jnp.int32, (size, pw), 1)
        left = (lane_s & (2 * size - 1)) < size
        x21 = {}
        for u in units:
            n21 = jnp.where(left, packed[2 * size][u][size:], 0.0)
            x21[u], = _dot3_rows([n21], *bd_parts(t_blk[u], blocks))
        grown = {}
        for u in units:
            x_hi, x_lo = _split(x21[u])
            z = jnp.zeros_like(x_hi)
            rhs_hi = jnp.concatenate([jnp.where(blocks[r - 1], x_hi, z) if r % 2 else z
                                      for r in range(pw // size)], axis=0)
            rhs_lo = jnp.concatenate([jnp.where(blocks[r - 1], x_lo, z) if r % 2 else z
                                      for r in range(pw // size)], axis=0)
            t21, = _dot3_rows([t_blk[u]], rhs_hi, rhs_lo)
            grown[u] = jnp.concatenate([jnp.where(left, t_blk[u], 0.0),
                                        jnp.where(left, t21, t_blk[u])], axis=0)
        t_blk = grown
        size *= 2
    t_inv = t_blk
    u_mat, w_mat = {}, {}
    for u in units:
        t_b = t_inv[u].astype(BF16)
        u_mat[u] = jnp.dot(t_b, pre[u]["vb_bd"], preferred_element_type=F32)
        w_mat[u] = jnp.dot(t_b, pre[u]["kbe_bd"], preferred_element_type=F32).astype(BF16)

    states = [state_ref[p] for p in range(n_pairs)]
    for ci in range(n_chunks):
        rows = slice(ci * c, (ci + 1) * c)
        for p in range(n_pairs):
            u = (ci, p)
            cols = slice(p * pw, (p + 1) * pw)
            state = states[p]
            ws_qs = jnp.dot(jnp.concatenate([w_mat[u], pre[u]["q_exp"]], axis=0), state.astype(BF16),
                            preferred_element_type=F32)
            v_new_b = (u_mat[u] - ws_qs[:c]).astype(BF16)
            out = ws_qs[c:] + jnp.dot(pre[u]["qk"], blockdiag(v_new_b), preferred_element_type=F32)
            kv = lax.dot_general(pre[u]["k_dec"], v_new_b, TN_DIMS, preferred_element_type=F32)
            states[p] = jnp.where(same_head, state * pre[u]["exp_last"] + kv, 0.0)
            sq = out * out
            ms = jnp.concatenate(
                [jnp.broadcast_to(jnp.mean(sq[:, :hd], axis=-1, keepdims=True), (c, hd)),
                 jnp.broadcast_to(jnp.mean(sq[:, hd:], axis=-1, keepdims=True), (c, hd))], axis=1)
            out = out * lax.rsqrt(ms + NORM_EPS) * norm_g * sz_ref[rows, cols].astype(F32)
            o_ref[rows, cols] = out.astype(BF16)
    for p in range(n_pairs):
        state_ref[p] = states[p]


def _delta(qkv, sz, gb, norm_g, expert_weights, *, bsz, seq, tc):
    dn_dim = sz.shape[1]
    nl = seq // tc
    tok = lambda b, i: (b * nl + i, 0)
    n_experts = expert_weights[0].shape[0]
    slabs = (bsz * nl) // n_experts
    assert slabs * n_experts == bsz * nl and all(w.shape[1] % (8 * slabs) == 0 for w in expert_weights)
    slab_spec = lambda w: pl.BlockSpec(
        (1, w.shape[1] // slabs, w.shape[2]), lambda b, i: ((b * nl + i) // slabs, (b * nl + i) % slabs, 0))
    return pl.pallas_call(
        functools.partial(_delta_kernel, tc=tc),
        grid=(bsz, nl),
        in_specs=[pl.BlockSpec((tc, dn_dim), lambda b, i: (b * nl + i, 0)),
                  pl.BlockSpec((tc, dn_dim), lambda b, i: (b * nl + i, 1)),
                  pl.BlockSpec((tc, dn_dim), lambda b, i: (b * nl + i, 2)),
                  pl.BlockSpec((tc, LANES), tok),
                  pl.BlockSpec((tc, dn_dim), tok),
                  _const_spec(norm_g.shape)] + [slab_spec(w) for w in expert_weights],
        out_specs=[pl.BlockSpec((tc, dn_dim), tok)] + [slab_spec(w) for w in expert_weights],
        out_shape=[jax.ShapeDtypeStruct((bsz * seq, dn_dim), BF16)]
        + [jax.ShapeDtypeStruct(w.shape, BF16) for w in expert_weights],
        scratch_shapes=[pltpu.VMEM((DN_HEADS // 2, 2 * DN_HEAD_DIM, 2 * DN_HEAD_DIM), F32)],
        compiler_params=_params(2),
        name="delta",
    )(qkv, qkv, qkv, gb, sz, norm_g, *expert_weights)


def _sgu_body(x_ref, wuv_ref, wgate_ref, lng_ref, lnb_ref, sw_ref, sb_ref, wso_ref, u_ref, v_ref, gated_ref,
              *, tm, sgu_dim, d_model):
    xb = x_ref[...].astype(BF16)
    inv_sqrt2 = 2.0 ** -0.5
    for n in range(0, 2 * sgu_dim, 256):
        uv = jnp.dot(xb, wuv_ref[:, n:n + 256], preferred_element_type=F32)
        act = 0.5 * uv * (1.0 + lax.erf(uv * inv_sqrt2))
        if n < sgu_dim:
            u_ref[:, n:n + 256] = act
        else:
            v_ref[:, n - sgu_dim:n - sgu_dim + 256] = act

    v = v_ref[...]
    vc = v - jnp.mean(v, axis=-1, keepdims=True)
    var = jnp.mean(vc * vc, axis=-1, keepdims=True)
    v_ref[...] = vc * lax.rsqrt(var + LN_EPS) * lng_ref[...] + lnb_ref[...]

    row = lax.broadcasted_iota(jnp.int32, (SGU_CHUNK, SGU_CHUNK), 0)
    col = lax.broadcasted_iota(jnp.int32, (SGU_CHUNK, SGU_CHUNK), 1)
    causal = row >= col
    for g in range(SGU_GROUPS):
        cols = slice(g * LANES, (g + 1) * LANES)
        w_causal = jnp.where(causal, sw_ref[g], 0.0).astype(BF16)
        bias = sb_ref[:, g:g + 1]
        for ci in range(tm // SGU_CHUNK):
            rows = slice(ci * SGU_CHUNK, (ci + 1) * SGU_CHUNK)
            mixed = jnp.dot(w_causal, v_ref[rows, cols].astype(BF16), preferred_element_type=F32) + bias
            gated_ref[rows, cols] = (u_ref[rows, cols] * mixed).astype(BF16)

    gated = gated_ref[...]
    ys, sgd = [], []
    for n in range(0, d_model, 256):
        y_sgu = jnp.dot(gated, wso_ref[:, n:n + 256], preferred_element_type=F32)
        gate_sgu = jnp.dot(xb, wgate_ref[:, d_model + n:d_model + n + 256], preferred_element_type=F32)
        ys.append(_sigmoid(gate_sgu) * y_sgu)
        gate_dn = jnp.dot(xb, wgate_ref[:, n:n + 256], preferred_element_type=F32)
        sgd.append(_sigmoid(gate_dn))
    return jnp.concatenate(ys, axis=1), jnp.concatenate(sgd, axis=1)


def _layer_norm(r, g, b):
    rc = r - jnp.mean(r, axis=-1, keepdims=True)
    var = jnp.mean(rc * rc, axis=-1, keepdims=True)
    return rc * lax.rsqrt(var + LN_EPS) * g + b


def _merge_kernel(x_ref, o_ref, wuv_ref, wgate_ref, lng_ref, lnb_ref, sw_ref, sb_ref, wso_ref,
                  wdn_ref, wout_ref, g_ref, b_ref, wr_ref, br_ref,
                  h_ref, rinfo_ref, dest_ref, nsub_ref, info_ref, u_ref, v_ref, gated_ref,
                  *, alpha, tm, sgu_dim, d_model):
    ys, sgd = _sgu_body(x_ref, wuv_ref, wgate_ref, lng_ref, lnb_ref, sw_ref, sb_ref, wso_ref,
                        u_ref, v_ref, gated_ref, tm=tm, sgu_dim=sgu_dim, d_model=d_model)
    y_dn = jnp.dot(o_ref[...], wdn_ref[...], preferred_element_type=F32)
    y = sgd * y_dn + ys
    mix = jnp.dot(y.astype(BF16), wout_ref[...], preferred_element_type=F32)
    h = _layer_norm(alpha * x_ref[...] + mix, g_ref[...], b_ref[...])
    h_ref[...] = h

    h_hi, h_lo = _split(h)
    parts = jnp.dot(jnp.concatenate([h_hi, h_lo], axis=0), wr_ref[...], preferred_element_type=F32)
    logits = (parts[:tm, :LANES] + parts[tm:, :LANES]) + (parts[:tm, LANES:] + parts[tm:, LANES:]) + br_ref[...]
    lt = logits.T
    row_of = lambda i: lt[i:i + 1, :]
    g_logit = [row_of(g) for g in range(N_GROUPS)]
    g_max = functools.reduce(jnp.maximum, g_logit)
    g_idx = jnp.full(g_max.shape, N_GROUPS - 1, jnp.int32)
    for g in range(N_GROUPS - 2, -1, -1):
        g_idx = jnp.where(g_logit[g] == g_max, g, g_idx)
    group_p = 1.0 / functools.reduce(jnp.add, [jnp.exp(gl - g_max) for gl in g_logit])
    e_logit = []
    for j in range(EXPERTS_PER_GROUP):
        sel = row_of(N_GROUPS + (N_GROUPS - 1) * EXPERTS_PER_GROUP + j)
        for g in range(N_GROUPS - 2, -1, -1):
            sel = jnp.where(g_idx == g, row_of(N_GROUPS + g * EXPERTS_PER_GROUP + j), sel)
        e_logit.append(sel)
    e1 = functools.reduce(jnp.maximum, e_logit)
    j1 = jnp.full(e1.shape, EXPERTS_PER_GROUP - 1, jnp.int32)
    for j in range(EXPERTS_PER_GROUP - 2, -1, -1):
        j1 = jnp.where(e_logit[j] == e1, j, j1)
    rest = [jnp.where(j1 == j, -jnp.inf, e_logit[j]) for j in range(EXPERTS_PER_GROUP)]
    e2 = functools.reduce(jnp.maximum, rest)
    j2 = jnp.full(e2.shape, EXPERTS_PER_GROUP - 1, jnp.int32)
    for j in range(EXPERTS_PER_GROUP - 2, -1, -1):
        j2 = jnp.where(rest[j] == e2, j, j2)
    r = jnp.exp(e2 - e1)
    p1 = group_p / (1.0 + r)
    p2 = group_p * r / (1.0 + r)
    p1_hi = p1.astype(BF16).astype(F32)
    p2_hi = p2.astype(BF16).astype(F32)

    onehot = [jnp.where(g_idx == g, 1.0, 0.0) for g in range(N_GROUPS)]
    trow = lax.broadcasted_iota(jnp.int32, (tm, tm), 0)
    tcol = lax.broadcasted_iota(jnp.int32, (tm, tm), 1)
    earlier = jnp.where(trow < tcol, 1.0, 0.0).astype(BF16)
    onehot_rows = jnp.concatenate(onehot + [jnp.zeros((8 - N_GROUPS, tm), F32)], axis=0).astype(BF16)
    rank = jnp.dot(onehot_rows, earlier, preferred_element_type=F32)
    lane1 = lax.broadcasted_iota(jnp.int32, (1, LANES), 1)
    nsub_row = jnp.zeros((1, LANES), F32)
    dest_row = jnp.zeros((1, tm), F32)
    start = jnp.zeros((1, 1), F32)
    for g in range(N_GROUPS):
        count = jnp.sum(onehot[g], axis=-1, keepdims=True)
        n_sub = jnp.floor((count + (SUB_ROWS - 1)) * (1.0 / SUB_ROWS))
        dest_row = dest_row + onehot[g] * (start * SUB_ROWS + rank[g:g + 1, :])
        nsub_row = jnp.where(lane1 == g, n_sub, jnp.where(lane1 == N_GROUPS + g, start, nsub_row))
        start = start + n_sub
    dest_ref[0] = dest_row.astype(jnp.int32)
    nsub_ref[0] = nsub_row.astype(jnp.int32)
    info_ref[...] = jnp.zeros(info_ref.shape, F32)
    for j in range(EXPERTS_PER_GROUP):
        info_ref[j:j + 1, :] = jnp.where(j1 == j, p1_hi, jnp.where(j2 == j, p2_hi, 0.0))
        info_ref[EXPERTS_PER_GROUP + j:EXPERTS_PER_GROUP + j + 1, :] = jnp.where(
            j1 == j, p1 - p1_hi, jnp.where(j2 == j, p2 - p2_hi, 0.0))
    info_ref[2 * EXPERTS_PER_GROUP:2 * EXPERTS_PER_GROUP + 1, :] = dest_row
    rinfo_ref[...] = info_ref[...].T


def _merge(x2, o, wuv, wgate, sgu_ln_g, sgu_ln_b, spatial_w, spatial_b_t, wso, wdn, wout, ln_g, ln_b, wr, br,
           *, tm, alpha):
    t, d = x2.shape
    sgu_dim = wso.shape[0]
    nb = t // tm
    tok = lambda i: (i, 0)
    consts = (wuv, wgate, sgu_ln_g, sgu_ln_b, spatial_w, spatial_b_t, wso, wdn, wout, ln_g, ln_b, wr, br)
    return pl.pallas_call(
        functools.partial(_merge_kernel, alpha=alpha, tm=tm, sgu_dim=sgu_dim, d_model=d),
        grid=(nb,),
        in_specs=[pl.BlockSpec((tm, d), tok), pl.BlockSpec((tm, o.shape[1]), tok)]
        + [_const_spec(c.shape) for c in consts],
        out_specs=[pl.BlockSpec((tm, d), tok), pl.BlockSpec((tm, LANES), tok),
                   pl.BlockSpec((1, 1, tm), lambda i: (i, 0, 0)),
                   pl.BlockSpec((1, 1, LANES), lambda i: (i, 0, 0))],
        out_shape=[jax.ShapeDtypeStruct((t, d), F32), jax.ShapeDtypeStruct((t, LANES), F32),
                   jax.ShapeDtypeStruct((nb, 1, tm), jnp.int32),
                   jax.ShapeDtypeStruct((nb, 1, LANES), jnp.int32)],
        scratch_shapes=[pltpu.VMEM((LANES, tm), F32),
                        pltpu.VMEM((tm, sgu_dim), F32), pltpu.VMEM((tm, sgu_dim), F32),
                        pltpu.VMEM((tm, sgu_dim), BF16)],
        compiler_params=_params(1),
        name="sgu_merge",
    )(x2, o, *consts)


def _moe_kernel(nsub_ref, h_ref, rinfo_ref, dest_ref, wg_ref, wu_ref, wd_ref, g_ref, b_ref, out_ref, acc_ref,
                *, alpha, tm):
    blk = pl.program_id(0)
    h = h_ref[...]
    hb = h.astype(BF16)
    rinfo = rinfo_ref[...]
    lane = lax.broadcasted_iota(jnp.int32, rinfo.shape, 1)
    cw_parts = jnp.where(lane < 2 * EXPERTS_PER_GROUP, rinfo, 0.0).astype(BF16)
    dest_col = rinfo[:, 2 * EXPERTS_PER_GROUP:2 * EXPERTS_PER_GROUP + 1].astype(jnp.int32)
    dest_row = dest_ref[0]
    sub_row = lax.broadcasted_iota(jnp.int32, (SUB_ROWS, tm), 0)
    sub_lane = lax.broadcasted_iota(jnp.int32, (tm, SUB_ROWS), 1)

    def sub_tiles(groups, first_rows):
        n = len(groups)
        gather = [jnp.where(sub_row + r == dest_row, 1.0, 0.0).astype(BF16) for r in first_rows]
        scatter = [jnp.where(sub_lane + r == dest_col, 1.0, 0.0).astype(BF16) for r in first_rows]
        xs = [jnp.dot(gather[i], hb, preferred_element_type=F32).astype(BF16) for i in range(n)]
        cws = [jnp.dot(gather[i], cw_parts, preferred_element_type=F32) for i in range(n)]
        y = [None] * n
        for j in range(EXPERTS_PER_GROUP):
            hg = [jnp.dot(xs[i], wg_ref[groups[i] * EXPERTS_PER_GROUP + j], preferred_element_type=F32)
                  for i in range(n)]
            hu = [jnp.dot(xs[i], wu_ref[groups[i] * EXPERTS_PER_GROUP + j], preferred_element_type=F32)
                  for i in range(n)]
            for i in range(n):
                c_e = cws[i][:, j:j + 1] + cws[i][:, EXPERTS_PER_GROUP + j:EXPERTS_PER_GROUP + j + 1]
                hm = (_silu(hg[i]) * hu[i] * c_e).astype(BF16)
                part = jnp.dot(hm, wd_ref[groups[i] * EXPERTS_PER_GROUP + j], preferred_element_type=F32)
                y[i] = part if y[i] is None else y[i] + part
        out = [jnp.dot(scatter[i], y[i].astype(BF16), preferred_element_type=F32) for i in range(n)]
        return functools.reduce(jnp.add, out)

    n_sub = [nsub_ref[blk * LANES + g] for g in range(N_GROUPS)]
    first = [nsub_ref[blk * LANES + N_GROUPS + g] for g in range(N_GROUPS)]
    past_end = (tm // SUB_ROWS + N_GROUPS) * SUB_ROWS
    acc_ref[...] = sub_tiles(list(range(N_GROUPS)),
                             [jnp.where(n_sub[g] > 0, first[g] * SUB_ROWS, past_end) for g in range(N_GROUPS)])

    for g in range(N_GROUPS):
        def later_sub_tile(s, carry, g=g):
            acc_ref[...] += sub_tiles([g], [(first[g] + s) * SUB_ROWS])
            return carry

        lax.fori_loop(1, n_sub[g], later_sub_tile, 0)

    out_ref[...] = _layer_norm(alpha * h + acc_ref[...], g_ref[...], b_ref[...])


def _moe(h1, rinfo, dest, nsub, wg, wu, wd, ln_g, ln_b, *, tm, alpha):
    t, d = h1.shape
    nd = lambda shape: pl.BlockSpec(shape, lambda i, ns: (0,) * len(shape), pipeline_mode=pl.Buffered(1))
    tok = lambda i, ns: (i, 0)
    grid_spec = pltpu.PrefetchScalarGridSpec(
        num_scalar_prefetch=1,
        grid=(t // tm,),
        in_specs=[pl.BlockSpec((tm, d), tok), pl.BlockSpec((tm, LANES), tok),
                  pl.BlockSpec((1, 1, tm), lambda i, ns: (i, 0, 0)),
                  nd(wg.shape), nd(wu.shape), nd(wd.shape), nd(ln_g.shape), nd(ln_b.shape)],
        out_specs=pl.BlockSpec((tm, d), tok),
        scratch_shapes=[pltpu.VMEM((tm, d), F32)])
    return pl.pallas_call(
        functools.partial(_moe_kernel, alpha=alpha, tm=tm),
        grid_spec=grid_spec,
        out_shape=jax.ShapeDtypeStruct((t, d), F32),
        compiler_params=_params(1),
        name="moe",
    )(nsub.reshape(-1), h1, rinfo, dest, wg, wu, wd, ln_g, ln_b)


def _pad_lanes(a, width=LANES):
    return jnp.pad(a, [(0, 0)] * (a.ndim - 1) + [(0, width - a.shape[-1])])


def _layer(h, w_in, conv_w, a_log, dt_bias, dn_norm_g, w_dn_out, sgu_ln_g, sgu_ln_b,
           spatial_w, spatial_b, w_sgu_out, w_out, ln1_g, ln1_b,
           router_group_w, router_group_b, router_expert_w, router_expert_b,
           expert_w_gate, expert_w_up, expert_w_down, ln2_g, ln2_b, *, alpha):
    bsz, seq, d = h.shape
    dn_dim = w_dn_out.shape[0]
    sgu_dim = w_sgu_out.shape[0]
    x2 = h.reshape(bsz * seq, d)

    o_z = 3 * dn_dim
    o_a = o_z + dn_dim
    o_uv = o_a + 2 * DN_HEADS
    o_gate = o_uv + 2 * sgu_dim
    wqkv = w_in[:, :o_z].astype(BF16)
    wz = w_in[:, o_z:o_a].astype(BF16)
    wab = _pad_lanes(w_in[:, o_a:o_uv]).astype(BF16)
    wuv = w_in[:, o_uv:o_gate].astype(BF16)
    wgate = w_in[:, o_gate:].astype(BF16)
    gpar = jnp.stack([_pad_lanes(-jnp.exp(a_log.astype(F32))), _pad_lanes(dt_bias.astype(F32))])

    qkv, sz, gb = _proj_dn(x2, wqkv, wz, wab, conv_w, gpar, bsz=bsz, seq=seq, tm=PROJ_TILE)
    o, wg, wu, wd = _delta(qkv, sz, gb, dn_norm_g.reshape(1, -1), (expert_w_gate, expert_w_up, expert_w_down),
                           bsz=bsz, seq=seq, tc=DELTA_TILE)
    wr = _pad_lanes(jnp.concatenate([router_group_w, router_expert_w], axis=1))
    wr = jnp.concatenate(_split(wr), axis=1)
    br = _pad_lanes(jnp.concatenate([router_group_b, router_expert_b]).reshape(1, -1))
    h1, rinfo, dest, nsub = _merge(x2, o, wuv, wgate, sgu_ln_g.reshape(1, -1), sgu_ln_b.reshape(1, -1),
                                   spatial_w, spatial_b.T, w_sgu_out.astype(BF16),
                                   w_dn_out.astype(BF16), w_out.astype(BF16),
                                   ln1_g.reshape(1, -1), ln1_b.reshape(1, -1), wr, br,
                                   tm=MOE_BLOCK, alpha=alpha)
    h2 = _moe(h1, rinfo, dest, nsub, wg, wu, wd, ln2_g.reshape(1, -1), ln2_b.reshape(1, -1),
              tm=MOE_BLOCK, alpha=alpha)
    return h2.reshape(bsz, seq, d)


def kernel(x, w_in, conv_w, a_log, dt_bias, dn_norm_g, w_dn_out, sgu_ln_g, sgu_ln_b, spatial_w, spatial_b, w_sgu_out, w_out, ln1_g, ln1_b, router_group_w, router_group_b, router_expert_w, router_expert_b, expert_w_gate, expert_w_up, expert_w_down, ln2_g, ln2_b):
    depth = w_in.shape[0]
    alpha = (2.0 * depth) ** 0.25
    h = x
    for l in range(depth):
        h = _layer(h, w_in[l], conv_w[l], a_log[l], dt_bias[l], dn_norm_g[l], w_dn_out[l],
                   sgu_ln_g[l], sgu_ln_b[l], spatial_w[l], spatial_b[l], w_sgu_out[l], w_out[l],
                   ln1_g[l], ln1_b[l], router_group_w[l], router_group_b[l],
                   router_expert_w[l], router_expert_b[l],
                   expert_w_gate[l], expert_w_up[l], expert_w_down[l], ln2_g[l], ln2_b[l], alpha=alpha)
    return h
```
